```python
import jax, jax.numpy as jnp
from jax import lax
import numpy as np

D_MODEL = 1024
BATCH = 16
SEQ = 4096
DEPTH = 2
DEC_BATCH = 8
DEC_SEQ = 16
PAST_LEN = 2048

CHUNK = 64
N_LEFT_CHUNKS = 8
BAND_PAST = N_LEFT_CHUNKS * CHUNK
A_HEADS = 8
A_HEAD_DIM = 64
A_WIDTH = A_HEADS * A_HEAD_DIM
MAX_REL = 128
B_WIDTH = D_MODEL - A_WIDTH
CONV_WIDTH = 31
C_HEADS = 8
C_KEY_DIM = 128
C_VAL_DIM = D_MODEL // C_HEADS
C_WIDTH = C_HEADS * C_VAL_DIM
D_FF = ((8 * D_MODEL // 3 + 255) // 256) * 256
N_EVEN = (DEPTH + 1) // 2
N_ODD = DEPTH // 2
EPS = 1e-6
NEG = -1e30

kernel_name = "hybrid_chunk_stream_encoder_step"


def rmsnorm(x, g):
    xf = x.astype(jnp.float32)
    y = xf * lax.rsqrt(jnp.mean(xf * xf, axis=-1, keepdims=True) + EPS)
    return (y * g.astype(jnp.float32)).astype(x.dtype)


def layernorm(x, g, b):
    xf = x.astype(jnp.float32)
    mu = jnp.mean(xf, axis=-1, keepdims=True)
    var = jnp.mean(jnp.square(xf - mu), axis=-1, keepdims=True)
    y = (xf - mu) * lax.rsqrt(var + EPS)
    return (y * g.astype(jnp.float32) + b.astype(jnp.float32)).astype(x.dtype)


def rel_bias_lookup(rel_bias, rel):
    return rel_bias[:, jnp.clip(rel, -MAX_REL, MAX_REL) + MAX_REL].astype(jnp.float32)


def band_attention_prompt(q, k, v, rel_bias):
    Bsz, T, H, Dh = q.shape
    nc = T // CHUNK
    band = BAND_PAST + CHUNK
    kp = jnp.pad(k, ((0, 0), (BAND_PAST, 0), (0, 0), (0, 0)))
    vp = jnp.pad(v, ((0, 0), (BAND_PAST, 0), (0, 0), (0, 0)))
    qc = q.reshape(Bsz, nc, CHUNK, H, Dh).swapaxes(0, 1)
    offs_q = jnp.arange(CHUNK)
    offs_k = jnp.arange(band)
    bias = rel_bias_lookup(rel_bias, offs_q[:, None] + BAND_PAST - offs_k[None, :])
    scale = A_HEAD_DIM ** -0.5

    def one_chunk(args):
        c, qb = args
        start = c * CHUNK
        kb = lax.dynamic_slice_in_dim(kp, start, band, axis=1)
        vb = lax.dynamic_slice_in_dim(vp, start, band, axis=1)
        valid = (start - BAND_PAST + offs_k) >= 0
        s = jnp.einsum('bqhd,bkhd->bhqk', qb, kb).astype(jnp.float32) * scale + bias[None]
        s = jnp.where(valid[None, None, None, :], s, NEG)
        p = jax.nn.softmax(s, axis=-1).astype(vb.dtype)
        return jnp.einsum('bhqk,bkhd->bqhd', p, vb)

    out = lax.map(one_chunk, (jnp.arange(nc), qc))
    return out.swapaxes(0, 1).reshape(Bsz, T, H, Dh)


def band_attention_sample(q, k_new, v_new, k_cache, v_cache, rel_bias):
    W = k_cache.shape[1]
    L = q.shape[1]
    k = jnp.concatenate([k_cache.astype(k_new.dtype), k_new], axis=1)
    v = jnp.concatenate([v_cache.astype(v_new.dtype), v_new], axis=1)
    rel = (W + jnp.arange(L))[:, None] - jnp.arange(W + L)[None, :]
    bias = rel_bias_lookup(rel_bias, rel)
    s = jnp.einsum('bqhd,bkhd->bhqk', q, k).astype(jnp.float32) * (A_HEAD_DIM ** -0.5) + bias[None]
    p = jax.nn.softmax(s, axis=-1).astype(v.dtype)
    return jnp.einsum('bhqk,bkhd->bqhd', p, v)


def conv_module(pre_glu, conv_hist, conv_w, conv_b, ln_g, ln_b):
    a, g = jnp.split(pre_glu, 2, axis=-1)
    u = a * jax.nn.sigmoid(g)
    u_ext = jnp.concatenate([conv_hist.astype(u.dtype), u], axis=1)
    y = lax.conv_general_dilated(u_ext, conv_w[:, None, :].astype(u.dtype), window_strides=(1,),
                                 padding='VALID', dimension_numbers=('NWC', 'WIO', 'NWC'),
                                 feature_group_count=B_WIDTH) + conv_b
    y = jax.nn.silu(layernorm(y, ln_g, ln_b))
    return y, u_ext[:, -(CONV_WIDTH - 1):]


def mixer_ab(h, k_cache, v_cache, conv_hist, w_in, rel_bias, conv_w, conv_b, ln_g, ln_b, w_out):
    Bsz, T, _ = h.shape
    z = h @ w_in
    q = z[..., :A_WIDTH].reshape(Bsz, T, A_HEADS, A_HEAD_DIM)
    k = z[..., A_WIDTH:2 * A_WIDTH].reshape(Bsz, T, A_HEADS, A_HEAD_DIM)
    v = z[..., 2 * A_WIDTH:3 * A_WIDTH].reshape(Bsz, T, A_HEADS, A_HEAD_DIM)
    pre_glu = z[..., 3 * A_WIDTH:]
    if k_cache is None:
        att = band_attention_prompt(q, k, v, rel_bias)
        keep = min(BAND_PAST, T)
        new_k, new_v = k[:, T - keep:], v[:, T - keep:]
        conv_hist = jnp.zeros((Bsz, CONV_WIDTH - 1, B_WIDTH), h.dtype)
    else:
        att = band_attention_sample(q, k, v, k_cache, v_cache, rel_bias)
        new_k, new_v = k, v
    conv_out, new_hist = conv_module(pre_glu, conv_hist, conv_w, conv_b, ln_g, ln_b)
    out = jnp.concatenate([att.reshape(Bsz, T, A_WIDTH), conv_out], axis=-1) @ w_out
    return out, new_k, new_v, new_hist


def hgrn2_chunk(S, q, k, v, logf):
    L = q.shape[1]
    G = jnp.cumsum(logf, axis=1)
    causal = jnp.tril(jnp.ones((L, L), dtype=bool))
    diff = G[:, :, None] - G[:, None, :]
    decay = jnp.exp(jnp.where(causal[None, :, :, None, None], diff, -jnp.inf))
    A = jnp.einsum('bthk,bshk,btshk->bhts', q, k, decay)
    o = jnp.einsum('bhts,bshv->bthv', A, v) + jnp.einsum('bthk,bhkv->bthv', q * jnp.exp(G), S)
    GL = G[:, -1]
    kd = k * jnp.exp(GL[:, None] - G)
    S_new = jnp.exp(GL)[..., None] * S + jnp.einsum('bshk,bshv->bhkv', kd, v)
    return S_new, o


def mixer_c(h, S0, w_in, lb, gn_g, w_out):
    Bsz, T, _ = h.shape
    z = h @ w_in
    qx, fx, ix, gx = jnp.split(z, 4, axis=-1)
    hk = (Bsz, T, C_HEADS, C_KEY_DIM)
    q = qx.reshape(hk).astype(jnp.float32)
    f = lb + (1.0 - lb) * jax.nn.sigmoid(fx.reshape(hk).astype(jnp.float32))
    logf = jnp.log(f)
    k = 1.0 - f
    v = ix.reshape(Bsz, T, C_HEADS, C_VAL_DIM).astype(jnp.float32)
    if S0 is None:
        nc = T // CHUNK

        def to_chunks(a):
            return a.reshape(Bsz, nc, CHUNK, *a.shape[2:]).swapaxes(0, 1)

        def step(S, xs):
            qc, kc, vc, lc = xs
            return hgrn2_chunk(S, qc, kc, vc, lc)

        S_init = jnp.zeros((Bsz, C_HEADS, C_KEY_DIM, C_VAL_DIM), jnp.float32)
        S, o = lax.scan(step, S_init, (to_chunks(q), to_chunks(k), to_chunks(v), to_chunks(logf)))
        o = o.swapaxes(0, 1).reshape(Bsz, T, C_HEADS, C_VAL_DIM)
    else:
        S, o = hgrn2_chunk(S0.astype(jnp.float32), q, k, v, logf)
    o = rmsnorm(o, gn_g).reshape(Bsz, T, C_WIDTH).astype(h.dtype) * jax.nn.silu(gx)
    return o @ w_out, S.astype(h.dtype)


def swiglu(h, w_in, w_out):
    a, b = jnp.split(h @ w_in, 2, axis=-1)
    return (jax.nn.silu(a) * b) @ w_out


def setup_inputs(seed: int = 0) -> dict:
    key = jax.random.key(seed)
    ks = jax.random.split(key, 24)
    f32 = jnp.float32
    nrm = lambda k, shp, s: jax.random.normal(k, shp, f32) * s
    wa = min(BAND_PAST, PAST_LEN)
    return {
        "x_prompt": nrm(ks[0], (BATCH, SEQ, D_MODEL), 1.0),
        "x_sample": nrm(ks[1], (DEC_BATCH, DEC_SEQ, D_MODEL), 1.0),
        "cache_a_k": nrm(ks[2], (N_EVEN, DEC_BATCH, wa, A_HEADS, A_HEAD_DIM), 1.0),
        "cache_a_v": nrm(ks[3], (N_EVEN, DEC_BATCH, wa, A_HEADS, A_HEAD_DIM), 1.0),
        "state_conv": nrm(ks[4], (N_EVEN, DEC_BATCH, CONV_WIDTH - 1, B_WIDTH), 0.5),
        "state_hgrn": nrm(ks[5], (N_ODD, DEC_BATCH, C_HEADS, C_KEY_DIM, C_VAL_DIM), 0.5),
        "attn_norm": 1.0 + nrm(ks[6], (DEPTH, D_MODEL), 0.02),
        "ffn_norm": 1.0 + nrm(ks[7], (DEPTH, D_MODEL), 0.02),
        "final_norm": 1.0 + nrm(ks[8], (D_MODEL,), 0.02),
        "w_in_ab": nrm(ks[9], (N_EVEN, D_MODEL, 3 * A_WIDTH + 2 * B_WIDTH), D_MODEL ** -0.5),
        "rel_bias": nrm(ks[10], (N_EVEN, A_HEADS, 2 * MAX_REL + 1), 0.1),
        "conv_w": nrm(ks[11], (N_EVEN, CONV_WIDTH, B_WIDTH), CONV_WIDTH ** -0.5),
        "conv_b": nrm(ks[12], (N_EVEN, B_WIDTH), 0.02),
        "conv_ln_g": 1.0 + nrm(ks[13], (N_EVEN, B_WIDTH), 0.02),
        "conv_ln_b": nrm(ks[14], (N_EVEN, B_WIDTH), 0.02),
        "w_out_ab": nrm(ks[15], (N_EVEN, D_MODEL, D_MODEL), D_MODEL ** -0.5),
        "w_in_c": nrm(ks[16], (N_ODD, D_MODEL, 4 * C_WIDTH), D_MODEL ** -0.5),
        "lower_bounds": nrm(ks[17], (DEPTH, C_HEADS * C_KEY_DIM), 1.0),
        "c_norm_g": 1.0 + nrm(ks[18], (N_ODD, C_VAL_DIM), 0.02),
        "w_out_c": nrm(ks[19], (N_ODD, C_WIDTH, D_MODEL), C_WIDTH ** -0.5),
        "ffn_w_in": nrm(ks[20], (DEPTH, D_MODEL, 2 * D_FF), D_MODEL ** -0.5),
        "ffn_w_out": nrm(ks[21], (DEPTH, D_FF, D_MODEL), D_FF ** -0.5),
    }


def reference(x_prompt, x_sample, cache_a_k, cache_a_v, state_conv, state_hgrn,
              attn_norm, ffn_norm, final_norm, w_in_ab, rel_bias, conv_w, conv_b,
              conv_ln_g, conv_ln_b, w_out_ab, w_in_c, lower_bounds, c_norm_g, w_out_c,
              ffn_w_in, ffn_w_out):
    lb_all = jnp.cumsum(jax.nn.softmax(lower_bounds.astype(jnp.float32), axis=0), axis=0)
    lb_all = lb_all - lb_all[0]
    yp, ys = x_prompt, x_sample
    akp, avp, cvp, hgp = [], [], [], []
    aks, avs, cvs, hgs = [], [], [], []
    for li in range(DEPTH):
        hp = rmsnorm(yp, attn_norm[li])
        hs = rmsnorm(ys, attn_norm[li])
        if li % 2 == 0:
            e = li // 2
            wts = (w_in_ab[e], rel_bias[e], conv_w[e], conv_b[e], conv_ln_g[e], conv_ln_b[e], w_out_ab[e])
            mp, kp_, vp_, cp_ = mixer_ab(hp, None, None, None, *wts)
            ms, ks_, vs_, cs_ = mixer_ab(hs, cache_a_k[e], cache_a_v[e], state_conv[e], *wts)
            akp.append(kp_); avp.append(vp_); cvp.append(cp_)
            aks.append(ks_); avs.append(vs_); cvs.append(cs_)
        else:
            o = li // 2
            lb = lb_all[li].reshape(C_HEADS, C_KEY_DIM)
            mp, sp_ = mixer_c(hp, None, w_in_c[o], lb, c_norm_g[o], w_out_c[o])
            ms, ss_ = mixer_c(hs, state_hgrn[o], w_in_c[o], lb, c_norm_g[o], w_out_c[o])
            hgp.append(sp_); hgs.append(ss_)
        yp = yp + mp
        ys = ys + ms
        yp = yp + swiglu(rmsnorm(yp, ffn_norm[li]), ffn_w_in[li], ffn_w_out[li])
        ys = ys + swiglu(rmsnorm(ys, ffn_norm[li]), ffn_w_in[li], ffn_w_out[li])
    y_prompt = rmsnorm(yp, final_norm)
    y_sample = rmsnorm(ys, final_norm)
    return (y_prompt, y_sample,
            jnp.stack(akp), jnp.stack(avp), jnp.stack(cvp), jnp.stack(hgp),
            jnp.stack(aks), jnp.stack(avs), jnp.stack(cvs), jnp.stack(hgs))
```

```python
import functools

import numpy as np
import jax
import jax.numpy as jnp
from jax import lax
from jax.experimental import pallas as pl
from jax.experimental.pallas import tpu as pltpu

EPS = 1e-6
NEG = -1e30
CHUNK = 64
BAND_PAST = 512
BAND = BAND_PAST + CHUNK
A_HEADS = 8
A_HEAD_DIM = 64
A_WIDTH = A_HEADS * A_HEAD_DIM
MAX_REL = 128
CONV_WIDTH = 31
C_HEADS = 8
C_DIM = 128
HIST_ROWS = CONV_WIDTH - 1
HIST_PAD = 32
HIST_OFF = HIST_PAD - HIST_ROWS
VMEM_LIMIT_BYTES = 56 * 1024 * 1024
BF16 = jnp.bfloat16
F32 = jnp.float32

_NT = (((1,), (1,)), ((), ()))
_TN = (((0,), (0,)), ((), ()))


def _rms(x, g):
    return x * lax.rsqrt(jnp.mean(x * x, axis=-1, keepdims=True) + EPS) * g


def _resident(shape):
    n = len(shape)
    return pl.BlockSpec(shape, lambda *_: (0,) * n, pipeline_mode=pl.Buffered(1))


def _params():
    return pltpu.CompilerParams(
        dimension_semantics=("arbitrary", "arbitrary"),
        vmem_limit_bytes=VMEM_LIMIT_BYTES,
    )


def _pick_tm(t):
    for tm in (256, 128, 64):
        if t % tm == 0:
            return tm
    return t


def _inproj_ab_kernel(x_ref, g_ref, w_ref, hist_ref, cw_ref, cb_ref, lng_ref, lnb_ref,
                      q_ref, k_ref, v_ref, kf_ref, vf_ref, co_ref, ut_ref, uext_ref,
                      *, tm, bw):
    j = pl.program_id(1)
    h = _rms(x_ref[0], g_ref[...]).astype(BF16)
    z = jnp.dot(h, w_ref[...], preferred_element_type=F32)
    q_ref[0] = (z[:, :A_WIDTH] * (A_HEAD_DIM ** -0.5)).astype(BF16)
    kk = z[:, A_WIDTH:2 * A_WIDTH]
    vv = z[:, 2 * A_WIDTH:3 * A_WIDTH]
    k_ref[0] = kk.astype(BF16)
    v_ref[0] = vv.astype(BF16)
    kf_ref[0] = kk
    vf_ref[0] = vv
    a = z[:, 3 * A_WIDTH:3 * A_WIDTH + bw]
    gate = z[:, 3 * A_WIDTH + bw:]
    u = a * jax.nn.sigmoid(gate)

    @pl.when(j == 0)
    def _():
        uext_ref[0:HIST_PAD, :] = hist_ref[0]

    uext_ref[HIST_PAD:HIST_PAD + tm, :] = u
    acc = jnp.broadcast_to(cb_ref[...], (tm, bw))
    for t in range(CONV_WIDTH):
        acc = acc + cw_ref[t:t + 1, :] * uext_ref[pl.ds(t + HIST_OFF, tm), :]
    mu = jnp.mean(acc, axis=-1, keepdims=True)
    d = acc - mu
    var = jnp.mean(d * d, axis=-1, keepdims=True)
    y = d * lax.rsqrt(var + EPS) * lng_ref[...] + lnb_ref[...]
    co_ref[0] = (y * jax.nn.sigmoid(y)).astype(BF16)
    tail = uext_ref[tm:tm + HIST_PAD, :]
    ut_ref[0] = tail
    uext_ref[0:HIST_PAD, :] = tail


def _inproj_ab(x, g, w, hist, cw, cb, lng, lnb):
    b, t, d = x.shape
    n = w.shape[1]
    bw = (n - 3 * A_WIDTH) // 2
    tm = _pick_tm(t)
    nt = t // tm
    keep = min(BAND_PAST, t)
    first_keep = nt - keep // tm
    tile = lambda c: pl.BlockSpec((1, tm, c), lambda i, j: (i, j, 0))
    keep_spec = pl.BlockSpec((1, tm, A_WIDTH), lambda i, j: (i, jnp.maximum(j - first_keep, 0), 0))
    return pl.pallas_call(
        functools.partial(_inproj_ab_kernel, tm=tm, bw=bw),
        grid=(b, nt),
        in_specs=[
            tile(d),
            _resident((1, d)),
            _resident((d, n)),
            pl.BlockSpec((1, HIST_PAD, bw), lambda i, j: (i, 0, 0)),
            _resident((HIST_PAD, bw)),
            _resident((1, bw)),
            _resident((1, bw)),
            _resident((1, bw)),
        ],
        out_specs=[
            tile(A_WIDTH), tile(A_WIDTH), tile(A_WIDTH),
            keep_spec, keep_spec,
            tile(bw),
            pl.BlockSpec((1, HIST_PAD, bw), lambda i, j: (i, 0, 0)),
        ],
        out_shape=[
            jax.ShapeDtypeStruct((b, t, A_WIDTH), BF16),
            jax.ShapeDtypeStruct((b, t, A_WIDTH), BF16),
            jax.ShapeDtypeStruct((b, t, A_WIDTH), BF16),
            jax.ShapeDtypeStruct((b, keep, A_WIDTH), F32),
            jax.ShapeDtypeStruct((b, keep, A_WIDTH), F32),
            jax.ShapeDtypeStruct((b, t, bw), BF16),
            jax.ShapeDtypeStruct((b, HIST_PAD, bw), F32),
        ],
        scratch_shapes=[pltpu.VMEM((tm + HIST_PAD, bw), F32)],
        compiler_params=_params(),
        name="inproj_ab",
    )(x, g, w, hist, cw, cb, lng, lnb)


def _attn_kernel(q_ref, k_ref, v_ref, b_ref, o_ref, *, mask_start):
    c = pl.program_id(1)
    start = pl.multiple_of(c * CHUNK, CHUNK)
    q = q_ref[0]
    kw = k_ref[0, pl.ds(start, BAND), :]
    vw = v_ref[0, pl.ds(start, BAND), :]
    if mask_start:
        col = lax.broadcasted_iota(jnp.int32, (1, BAND), 1)
        valid = (c * CHUNK - BAND_PAST + col) >= 0
    outs = []
    for h in range(A_HEADS):
        sl = slice(h * A_HEAD_DIM, (h + 1) * A_HEAD_DIM)
        s = lax.dot_general(q[:, sl], kw[:, sl], _NT, preferred_element_type=F32)
        s = s + b_ref[h]
        if mask_start:
            s = jnp.where(valid, s, NEG)
        m = jnp.max(s, axis=-1, keepdims=True)
        p = jnp.exp(s - m)
        l = jnp.sum(p, axis=-1, keepdims=True)
        o = jnp.dot(p.astype(BF16), vw[:, sl], preferred_element_type=F32)
        outs.append(o / l)
    o_ref[0] = jnp.concatenate(outs, axis=-1).astype(BF16)


def _attention(q, kp, vp, bias, mask_start):
    b, tq, _ = q.shape
    nc = tq // CHUNK
    tk = kp.shape[1]
    return pl.pallas_call(
        functools.partial(_attn_kernel, mask_start=mask_start),
        grid=(b, nc),
        in_specs=[
            pl.BlockSpec((1, CHUNK, A_WIDTH), lambda i, j: (i, j, 0)),
            pl.BlockSpec((1, tk, A_WIDTH), lambda i, j: (i, 0, 0)),
            pl.BlockSpec((1, tk, A_WIDTH), lambda i, j: (i, 0, 0)),
            _resident((A_HEADS, CHUNK, BAND)),
        ],
        out_specs=pl.BlockSpec((1, CHUNK, A_WIDTH), lambda i, j: (i, j, 0)),
        out_shape=jax.ShapeDtypeStruct((b, tq, A_WIDTH), BF16),
        compiler_params=_params(),
        name="band_attention",
    )(q, kp, vp, bias)


def _outproj_ffn_kernel(*refs, n_in, final):
    y_ref = refs[0]
    ins = refs[1:1 + 2 * n_in]
    fg_ref, w1_ref, w2_ref = refs[1 + 2 * n_in:4 + 2 * n_in]
    fn_ref = refs[4 + 2 * n_in] if final else None
    o_ref = refs[-1]
    y = y_ref[0]
    for i in range(n_in):
        y = y + jnp.dot(ins[2 * i][0], ins[2 * i + 1][...], preferred_element_type=F32)
    hn = _rms(y, fg_ref[...]).astype(BF16)
    hh = jnp.dot(hn, w1_ref[...], preferred_element_type=F32)
    dff = w2_ref.shape[0]
    a = hh[:, :dff]
    act = (a * jax.nn.sigmoid(a) * hh[:, dff:]).astype(BF16)
    y = y + jnp.dot(act, w2_ref[...], preferred_element_type=F32)
    if final:
        y = _rms(y, fn_ref[...])
    o_ref[0] = y


def _outproj_ffn(y, ins, fg, w1, w2, fn=None):
    b, t, d = y.shape
    tm = _pick_tm(t)
    tile = lambda c: pl.BlockSpec((1, tm, c), lambda i, j: (i, j, 0))
    in_specs = [tile(d)]
    args = [y]
    for a, w in ins:
        in_specs += [tile(a.shape[-1]), _resident(w.shape)]
        args += [a, w]
    in_specs += [_resident((1, d)), _resident(w1.shape), _resident(w2.shape)]
    args += [fg, w1, w2]
    if fn is not None:
        in_specs.append(_resident((1, d)))
        args.append(fn)
    return pl.pallas_call(
        functools.partial(_outproj_ffn_kernel, n_in=len(ins), final=fn is not None),
        grid=(b, t // tm),
        in_specs=in_specs,
        out_specs=tile(d),
        out_shape=jax.ShapeDtypeStruct((b, t, d), F32),
        compiler_params=_params(),
        name="outproj_ffn",
    )(*args)


def _hgrn_tables(length):
    nlev = int(np.log2(length))
    assert 2 ** nlev == length
    r = np.arange(length)
    sums = np.zeros((nlev + 1, length, length), np.float32)
    qmask = np.zeros((nlev, length, C_DIM), np.float32)
    pmask = np.zeros((nlev + 1, length, length), np.float32)
    for lv in range(nlev):
        hs = length >> (lv + 1)
        blk = r // (2 * hs)
        bnd = blk * 2 * hs + hs - 1
        is_q = (r % (2 * hs)) >= hs
        c = r[None, :]
        q_rows = (c > bnd[:, None]) & (c <= r[:, None])
        k_rows = (c > r[:, None]) & (c <= bnd[:, None])
        sums[lv] = np.where(is_q[:, None], q_rows, k_rows)
        qmask[lv] = is_q[:, None]
        pmask[lv] = blk[:, None] == blk[None, :]
    sums[nlev] = r[None, :] <= r[:, None]
    pmask[nlev] = np.eye(length)
    return nlev, sums.reshape((nlev + 1) * length, length), qmask, pmask


def _hgrn_kernel(x_ref, g_ref, w_ref, lbp_ref, s0_ref, gn_ref, sums_ref, qm_ref, pm_ref,
                 og_ref, sout_ref, z_scr, st_scr, *, tm, length, nlev, width):
    j = pl.program_id(1)
    h = _rms(x_ref[0], g_ref[...]).astype(BF16)
    z_scr[...] = jnp.dot(h, w_ref[...], preferred_element_type=F32)

    @pl.when(j == 0)
    def _():
        for hd in range(C_HEADS):
            st_scr[hd] = s0_ref[0, hd].T

    lbp = lbp_ref[...]
    mx = jnp.max(lbp, axis=0, keepdims=True)
    e = jnp.exp(lbp - mx)
    den = e[0:1] + e[1:2]
    s_first = e[0:1] / den
    lb = (s_first + e[1:2] / den) - s_first
    sums = sums_ref[...]
    gn = gn_ref[...]

    def chunk(ci, carry):
        r0 = pl.multiple_of(ci * length, length)
        rows = pl.ds(r0, length)
        for hd in range(C_HEADS):
            c0 = hd * C_DIM
            q = z_scr[rows, c0:c0 + C_DIM]
            fx = z_scr[rows, width + c0:width + c0 + C_DIM]
            v = z_scr[rows, 2 * width + c0:2 * width + c0 + C_DIM]
            gx = z_scr[rows, 3 * width + c0:3 * width + c0 + C_DIM]
            lbh = lb[:, c0:c0 + C_DIM]
            f = lbh + (1.0 - lbh) * jax.nn.sigmoid(fx)
            logf = jnp.log(f)
            kk = 1.0 - f
            hi = logf.astype(BF16)
            r1 = logf - hi.astype(F32)
            mid = r1.astype(BF16)
            lo = (r1 - mid.astype(F32)).astype(BF16)
            dall = (jnp.dot(sums, hi, preferred_element_type=F32)
                    + jnp.dot(sums, mid, preferred_element_type=F32)
                    + jnp.dot(sums, lo, preferred_element_type=F32))
            gcum = dall[nlev * length:(nlev + 1) * length]
            vb = v.astype(BF16)
            amat = lax.dot_general(q.astype(BF16), kk.astype(BF16), _NT,
                                   preferred_element_type=F32) * pm_ref[nlev]
            for lv in range(nlev):
                ef = jnp.exp(dall[lv * length:(lv + 1) * length])
                eq = ef * qm_ref[lv]
                ek = ef - eq
                p = lax.dot_general((q * eq).astype(BF16), (kk * ek).astype(BF16), _NT,
                                    preferred_element_type=F32)
                amat = amat + p * pm_ref[lv]
            st = st_scr[hd]
            o = (jnp.dot(amat.astype(BF16), vb, preferred_element_type=F32)
                 + lax.dot_general((q * jnp.exp(gcum)).astype(BF16), st.astype(BF16), _NT,
                                   preferred_element_type=F32))
            glast = gcum[length - 1:length, :]
            kd = kk * jnp.exp(glast - gcum)
            st_scr[hd] = st * jnp.exp(glast) + lax.dot_general(
                vb, kd.astype(BF16), _TN, preferred_element_type=F32)
            on = _rms(o, gn)
            og_ref[0, rows, c0:c0 + C_DIM] = (on * (gx * jax.nn.sigmoid(gx))).astype(BF16)
        return carry

    lax.fori_loop(0, tm // length, chunk, 0)

    @pl.when(j == pl.num_programs(1) - 1)
    def _():
        for hd in range(C_HEADS):
            sout_ref[0, hd] = st_scr[hd].T


def _hgrn(x, g, w, lbp, s0, gn):
    b, t, d = x.shape
    width = w.shape[1] // 4
    tm = _pick_tm(t)
    length = min(CHUNK, t)
    nlev, sums, qmask, pmask = _hgrn_tables(length)
    tile = lambda c: pl.BlockSpec((1, tm, c), lambda i, j: (i, j, 0))
    state = pl.BlockSpec((1, C_HEADS, C_DIM, C_DIM), lambda i, j: (i, 0, 0, 0))
    return pl.pallas_call(
        functools.partial(_hgrn_kernel, tm=tm, length=length, nlev=nlev, width=width),
        grid=(b, t // tm),
        in_specs=[
            tile(d),
            _resident((1, d)),
            _resident(w.shape),
            _resident(lbp.shape),
            state,
            _resident((1, C_DIM)),
            _resident(sums.shape),
            _resident(qmask.shape),
            _resident(pmask.shape),
        ],
        out_specs=[tile(width), state],
        out_shape=[
            jax.ShapeDtypeStruct((b, t, width), BF16),
            jax.ShapeDtypeStruct((b, C_HEADS, C_DIM, C_DIM), F32),
        ],
        scratch_shapes=[
            pltpu.VMEM((tm, 4 * width), F32),
            pltpu.VMEM((C_HEADS, C_DIM, C_DIM), F32),
        ],
        compiler_params=_params(),
        name="inproj_hgrn",
    )(x, g, w, lbp, s0, gn, jnp.asarray(sums, BF16), jnp.asarray(qmask), jnp.asarray(pmask))


def _rel_bias_table(rel_bias):
    rel = np.arange(CHUNK)[:, None] + BAND_PAST - np.arange(BAND)[None, :]
    idx = np.clip(rel, -MAX_REL, MAX_REL) + MAX_REL
    return rel_bias[:, idx].astype(F32)


def _stream(x, cache_k, cache_v, conv_hist, s0, p):
    b, t, d = x.shape
    bw = p["conv_w"].shape[-1]
    row = lambda a: a.reshape(1, -1)
    if cache_k is None:
        hist = jnp.zeros((b, HIST_PAD, bw), F32)
    else:
        hist = jnp.pad(conv_hist, ((0, 0), (HIST_OFF, 0), (0, 0)))
    q, k, v, kf, vf, co, ut = _inproj_ab(
        x, row(p["attn_norm"][0]), p["w_in_ab"], hist, p["conv_w"], row(p["conv_b"]),
        row(p["conv_ln_g"]), row(p["conv_ln_b"]))
    if cache_k is None:
        kp = jnp.pad(k, ((0, 0), (BAND_PAST, 0), (0, 0)))
        vp = jnp.pad(v, ((0, 0), (BAND_PAST, 0), (0, 0)))
        att = _attention(q, kp, vp, p["bias"], True)
    else:
        w = cache_k.shape[1]
        assert w == BAND_PAST and t <= CHUNK
        pad = ((0, 0), (0, CHUNK - t), (0, 0))
        kp = jnp.concatenate([cache_k.reshape(b, w, A_WIDTH).astype(BF16), jnp.pad(k, pad)], axis=1)
        vp = jnp.concatenate([cache_v.reshape(b, w, A_WIDTH).astype(BF16), jnp.pad(v, pad)], axis=1)
        bias = jnp.where(np.arange(BAND)[None, None, :] < w + t, p["bias"], NEG)
        att = _attention(jnp.pad(q, pad), kp, vp, bias, False)[:, :t]
    y = _outproj_ffn(x, [(att, p["w_out_ab"][:A_WIDTH]), (co, p["w_out_ab"][A_WIDTH:])],
                     row(p["ffn_norm"][0]), p["ffn_w_in"][0], p["ffn_w_out"][0])
    keep = kf.shape[1]
    new_k = kf.reshape(1, b, keep, A_HEADS, A_HEAD_DIM)
    new_v = vf.reshape(1, b, keep, A_HEADS, A_HEAD_DIM)
    new_conv = ut[None, :, HIST_OFF:]
    if s0 is None:
        s0 = jnp.zeros((b, C_HEADS, C_DIM, C_DIM), F32)
    og, s_new = _hgrn(y, row(p["attn_norm"][1]), p["w_in_c"], p["lower_bounds"], s0,
                      row(p["c_norm_g"]))
    y = _outproj_ffn(y, [(og, p["w_out_c"])], row(p["ffn_norm"][1]), p["ffn_w_in"][1],
                     p["ffn_w_out"][1], fn=row(p["final_norm"]))
    return y, new_k, new_v, new_conv, s_new[None]


def kernel(x_prompt, x_sample, cache_a_k, cache_a_v, state_conv, state_hgrn, attn_norm, ffn_norm, final_norm, w_in_ab, rel_bias, conv_w, conv_b, conv_ln_g, conv_ln_b, w_out_ab, w_in_c, lower_bounds, c_norm_g, w_out_c, ffn_w_in, ffn_w_out):
    assert attn_norm.shape[0] == 2 and w_in_ab.shape[0] == 1 and w_in_c.shape[0] == 1
    p = dict(
        attn_norm=attn_norm, ffn_norm=ffn_norm, final_norm=final_norm,
        w_in_ab=w_in_ab[0].astype(BF16),
        bias=_rel_bias_table(rel_bias[0]),
        conv_w=jnp.pad(conv_w[0], ((0, HIST_PAD - CONV_WIDTH), (0, 0))),
        conv_b=conv_b[0], conv_ln_g=conv_ln_g[0], conv_ln_b=conv_ln_b[0],
        w_out_ab=w_out_ab[0].astype(BF16),
        w_in_c=w_in_c[0].astype(BF16),
        lower_bounds=lower_bounds,
        c_norm_g=c_norm_g[0],
        w_out_c=w_out_c[0].astype(BF16),
        ffn_w_in=ffn_w_in.astype(BF16),
        ffn_w_out=ffn_w_out.astype(BF16),
    )
    yp, akp, avp, cvp, hgp = _stream(x_prompt, None, None, None, None, p)
    ys, aks, avs, cvs, hgs = _stream(x_sample, cache_a_k[0], cache_a_v[0], state_conv[0],
                                     state_hgrn[0], p)
    return (yp, ys, akp, avp, cvp, hgp, aks, avs, cvs, hgs)
```

```python
import functools

import numpy as np
import jax
import jax.numpy as jnp
from jax import lax
from jax.experimental import pallas as pl
from jax.experimental.pallas import tpu as pltpu

EPS = 1e-6
NEG = -1e30
CHUNK = 64
BAND_PAST = 512
BAND = BAND_PAST + CHUNK
Q_ROWS = 4 * CHUNK
WIN = BAND_PAST + Q_ROWS
A_HEADS = 8
A_HEAD_DIM = 64
A_WIDTH = A_HEADS * A_HEAD_DIM
MAX_REL = 128
CONV_WIDTH = 31
C_HEADS = 8
C_DIM = 128
HIST_ROWS = CONV_WIDTH - 1
HIST_PAD = 32
HIST_OFF = HIST_PAD - HIST_ROWS
VMEM_LIMIT_BYTES = 56 * 1024 * 1024
BF16 = jnp.bfloat16
F32 = jnp.float32

_NT = (((1,), (1,)), ((), ()))
_TN = (((0,), (0,)), ((), ()))


def _rms(x, g):
    return x * lax.rsqrt(jnp.mean(x * x, axis=-1, keepdims=True) + EPS) * g


def _resident(shape):
    n = len(shape)
    return pl.BlockSpec(shape, lambda *_: (0,) * n, pipeline_mode=pl.Buffered(1))


def _params():
    return pltpu.CompilerParams(
        dimension_semantics=("arbitrary", "arbitrary"),
        vmem_limit_bytes=VMEM_LIMIT_BYTES,
    )


def _pick_tm(t):
    for tm in (256, 128, 64):
        if t % tm == 0:
            return tm
    return t


def _inproj_ab_kernel(x_ref, g_ref, w_ref, hist_ref, cw_ref, cb_ref, lng_ref, lnb_ref,
                      q_ref, k_ref, v_ref, kf_ref, vf_ref, co_ref, ut_ref, uext_ref,
                      *, tm, bw):
    j = pl.program_id(1)
    h = _rms(x_ref[0], g_ref[...]).astype(BF16)
    z = jnp.dot(h, w_ref[...], preferred_element_type=F32)
    q_ref[0] = (z[:, :A_WIDTH] * (A_HEAD_DIM ** -0.5)).astype(BF16)
    kk = z[:, A_WIDTH:2 * A_WIDTH]
    vv = z[:, 2 * A_WIDTH:3 * A_WIDTH]
    k_ref[0] = kk.astype(BF16)
    v_ref[0] = vv.astype(BF16)
    kf_ref[0] = kk
    vf_ref[0] = vv
    a = z[:, 3 * A_WIDTH:3 * A_WIDTH + bw]
    gate = z[:, 3 * A_WIDTH + bw:]
    u = a * jax.nn.sigmoid(gate)

    @pl.when(j == 0)
    def _():
        uext_ref[0:HIST_PAD, :] = hist_ref[0]

    uext_ref[HIST_PAD:HIST_PAD + tm, :] = u
    acc = jnp.broadcast_to(cb_ref[...], (tm, bw))
    for t in range(CONV_WIDTH):
        acc = acc + cw_ref[t:t + 1, :] * uext_ref[pl.ds(t + HIST_OFF, tm), :]
    mu = jnp.mean(acc, axis=-1, keepdims=True)
    d = acc - mu
    var = jnp.mean(d * d, axis=-1, keepdims=True)
    y = d * lax.rsqrt(var + EPS) * lng_ref[...] + lnb_ref[...]
    co_ref[0] = (y * jax.nn.sigmoid(y)).astype(BF16)
    tail = uext_ref[tm:tm + HIST_PAD, :]
    ut_ref[0] = tail
    uext_ref[0:HIST_PAD, :] = tail


def _inproj_ab(x, g, w, hist, cw, cb, lng, lnb):
    b, t, d = x.shape
    n = w.shape[1]
    bw = (n - 3 * A_WIDTH) // 2
    tm = _pick_tm(t)
    nt = t // tm
    keep = min(BAND_PAST, t)
    first_keep = nt - keep // tm
    tile = lambda c: pl.BlockSpec((1, tm, c), lambda i, j: (i, j, 0))
    keep_spec = pl.BlockSpec((1, tm, A_WIDTH), lambda i, j: (i, jnp.maximum(j - first_keep, 0), 0))
    return pl.pallas_call(
        functools.partial(_inproj_ab_kernel, tm=tm, bw=bw),
        grid=(b, nt),
        in_specs=[
            tile(d),
            _resident((1, d)),
            _resident((d, n)),
            pl.BlockSpec((1, HIST_PAD, bw), lambda i, j: (i, 0, 0)),
            _resident((HIST_PAD, bw)),
            _resident((1, bw)),
            _resident((1, bw)),
            _resident((1, bw)),
        ],
        out_specs=[
            tile(A_WIDTH), tile(A_WIDTH), tile(A_WIDTH),
            keep_spec, keep_spec,
            tile(bw),
            pl.BlockSpec((1, HIST_PAD, bw), lambda i, j: (i, 0, 0)),
        ],
        out_shape=[
            jax.ShapeDtypeStruct((b, t, A_WIDTH), BF16),
            jax.ShapeDtypeStruct((b, t, A_WIDTH), BF16),
            jax.ShapeDtypeStruct((b, t, A_WIDTH), BF16),
            jax.ShapeDtypeStruct((b, keep, A_WIDTH), F32),
            jax.ShapeDtypeStruct((b, keep, A_WIDTH), F32),
            jax.ShapeDtypeStruct((b, t, bw), BF16),
            jax.ShapeDtypeStruct((b, HIST_PAD, bw), F32),
        ],
        scratch_shapes=[pltpu.VMEM((tm + HIST_PAD, bw), F32)],
        compiler_params=_params(),
        name="inproj_ab",
    )(x, g, w, hist, cw, cb, lng, lnb)


def _attn_group(q_ref, k_ref, v_ref, b_ref, o_ref, kstart, wlen):
    q = q_ref[0]
    kw = k_ref[0, pl.ds(kstart, wlen), :]
    vw = v_ref[0, pl.ds(kstart, wlen), :]
    outs = []
    for h in range(A_HEADS):
        sl = slice(h * A_HEAD_DIM, (h + 1) * A_HEAD_DIM)
        s = lax.dot_general(q[:, sl], kw[:, sl], _NT, preferred_element_type=F32)
        s = s + b_ref[h, :, WIN - wlen:]
        m = jnp.max(s, axis=-1, keepdims=True)
        p = jnp.exp(s - m)
        l = jnp.sum(p, axis=-1, keepdims=True)
        o = jnp.dot(p.astype(BF16), vw[:, sl], preferred_element_type=F32)
        outs.append(o / l)
    o_ref[0] = jnp.concatenate(outs, axis=-1).astype(BF16)


def _attn_kernel(q_ref, k_ref, v_ref, b_ref, o_ref, *, lead):
    g = pl.program_id(1)
    n_short = (BAND_PAST - lead) // Q_ROWS
    for i in range(n_short):
        @pl.when(g == i)
        def _(i=i):
            _attn_group(q_ref, k_ref, v_ref, b_ref, o_ref, 0, lead + (i + 1) * Q_ROWS)

    @pl.when(g >= n_short)
    def _():
        kstart = pl.multiple_of((g - n_short) * Q_ROWS, Q_ROWS)
        _attn_group(q_ref, k_ref, v_ref, b_ref, o_ref, kstart, WIN)


def _attention(q, k, v, bias, lead):
    b, tq, _ = q.shape
    tk = k.shape[1]
    assert tq % Q_ROWS == 0 and tk == lead + tq and (BAND_PAST - lead) % Q_ROWS == 0
    return pl.pallas_call(
        functools.partial(_attn_kernel, lead=lead),
        grid=(b, tq // Q_ROWS),
        in_specs=[
            pl.BlockSpec((1, Q_ROWS, A_WIDTH), lambda i, j: (i, j, 0)),
            pl.BlockSpec((1, tk, A_WIDTH), lambda i, j: (i, 0, 0)),
            pl.BlockSpec((1, tk, A_WIDTH), lambda i, j: (i, 0, 0)),
            _resident((A_HEADS, Q_ROWS, WIN)),
        ],
        out_specs=pl.BlockSpec((1, Q_ROWS, A_WIDTH), lambda i, j: (i, j, 0)),
        out_shape=jax.ShapeDtypeStruct((b, tq, A_WIDTH), BF16),
        compiler_params=_params(),
        name="band_attention",
    )(q, k, v, bias)


def _outproj_ffn_kernel(*refs, n_in, final):
    y_ref = refs[0]
    ins = refs[1:1 + 2 * n_in]
    fg_ref, w1_ref, w2_ref = refs[1 + 2 * n_in:4 + 2 * n_in]
    fn_ref = refs[4 + 2 * n_in] if final else None
    o_ref = refs[-1]
    y = y_ref[0]
    for i in range(n_in):
        y = y + jnp.dot(ins[2 * i][0], ins[2 * i + 1][...], preferred_element_type=F32)
    hn = _rms(y, fg_ref[...]).astype(BF16)
    hh = jnp.dot(hn, w1_ref[...], preferred_element_type=F32)
    dff = w2_ref.shape[0]
    a = hh[:, :dff]
    act = (a * jax.nn.sigmoid(a) * hh[:, dff:]).astype(BF16)
    y = y + jnp.dot(act, w2_ref[...], preferred_element_type=F32)
    if final:
        y = _rms(y, fn_ref[...])
    o_ref[0] = y


def _outproj_ffn(y, ins, fg, w1, w2, fn=None):
    b, t, d = y.shape
    tm = _pick_tm(t)
    tile = lambda c: pl.BlockSpec((1, tm, c), lambda i, j: (i, j, 0))
    in_specs = [tile(d)]
    args = [y]
    for a, w in ins:
        in_specs += [tile(a.shape[-1]), _resident(w.shape)]
        args += [a, w]
    in_specs += [_resident((1, d)), _resident(w1.shape), _resident(w2.shape)]
    args += [fg, w1, w2]
    if fn is not None:
        in_specs.append(_resident((1, d)))
        args.append(fn)
    return pl.pallas_call(
        functools.partial(_outproj_ffn_kernel, n_in=len(ins), final=fn is not None),
        grid=(b, t // tm),
        in_specs=in_specs,
        out_specs=tile(d),
        out_shape=jax.ShapeDtypeStruct((b, t, d), F32),
        compiler_params=_params(),
        name="outproj_ffn",
    )(*args)


def _hgrn_tables(length, width):
    nlev = int(np.log2(length))
    assert 2 ** nlev == length
    r = np.arange(length)
    sums = np.zeros((nlev + 1, length, length), np.float32)
    qmask = np.zeros((nlev, length, width), np.float32)
    pmask = np.zeros((nlev + 1, length, length), np.float32)
    for lv in range(nlev):
        hs = length >> (lv + 1)
        blk = r // (2 * hs)
        bnd = blk * 2 * hs + hs - 1
        is_q = (r % (2 * hs)) >= hs
        c = r[None, :]
        q_rows = (c > bnd[:, None]) & (c <= r[:, None])
        k_rows = (c > r[:, None]) & (c <= bnd[:, None])
        sums[lv] = np.where(is_q[:, None], q_rows, k_rows)
        qmask[lv] = is_q[:, None]
        pmask[lv] = (blk[:, None] == blk[None, :]) & is_q[:, None] & ~is_q[None, :]
    sums[nlev] = r[None, :] <= r[:, None]
    pmask[nlev] = np.eye(length)
    sums = sums.reshape((nlev + 1) * length, length)
    return (nlev, np.concatenate([sums, sums], axis=1), qmask,
            np.concatenate([pmask, pmask], axis=2))


def _pair_blocks(a):
    z = jnp.zeros((a.shape[0], C_DIM), a.dtype)
    return jnp.concatenate([jnp.concatenate([a[:, :C_DIM], z], axis=1),
                            jnp.concatenate([z, a[:, C_DIM:]], axis=1)], axis=0)


def _hgrn_kernel(x_ref, g_ref, w_ref, lbp_ref, s0_ref, gn_ref, sums_ref, qm_ref, pm_ref,
                 og_ref, sout_ref, z_scr, st_scr, *, tm, length, nlev, width):
    j = pl.program_id(1)
    h = _rms(x_ref[0], g_ref[...]).astype(BF16)
    z_scr[...] = jnp.dot(h, w_ref[...], preferred_element_type=F32)

    @pl.when(j == 0)
    def _():
        for hd in range(C_HEADS):
            st_scr[hd] = s0_ref[0, hd].T

    lbp = lbp_ref[...]
    mx = jnp.max(lbp, axis=0, keepdims=True)
    e = jnp.exp(lbp - mx)
    den = e[0:1] + e[1:2]
    s_first = e[0:1] / den
    lb = (s_first + e[1:2] / den) - s_first
    pair_w = 2 * C_DIM

    def chunk(ci, carry):
        r0 = pl.multiple_of(ci * length, length)
        rows = pl.ds(r0, length)
        q = z_scr[rows, 0:width]
        f = lb + (1.0 - lb) * jax.nn.sigmoid(z_scr[rows, width:2 * width])
        v = z_scr[rows, 2 * width:3 * width]
        logf = jnp.log(f)
        kk = 1.0 - f
        hi = logf.astype(BF16)
        mid = (logf - hi.astype(F32)).astype(BF16)
        dall = jnp.dot(sums_ref[...], jnp.concatenate([hi, mid], axis=0),
                       preferred_element_type=F32)
        gcum = dall[nlev * length:(nlev + 1) * length]
        qb = q.astype(BF16)
        kb = kk.astype(BF16)
        vb = v.astype(BF16)
        xs = []
        for lv in range(nlev):
            hs = length >> (lv + 1)
            ef = jnp.exp(dall[lv * length:(lv + 1) * length])
            if hs % 8 == 0:
                qk = jnp.concatenate(
                    [(q if i % 2 else kk)[i * hs:(i + 1) * hs] for i in range(length // hs)], axis=0)
            else:
                qk = jnp.where(qm_ref[lv] != 0.0, q, kk)
            xs.append((ef * qk).astype(BF16))
        qt = (q * jnp.exp(gcum)).astype(BF16)
        outs = []
        for pr in range(C_HEADS // 2):
            ps = slice(pr * pair_w, (pr + 1) * pair_w)
            amat = lax.dot_general(qb[:, ps], _pair_blocks(kb[:, ps]), _NT,
                                   preferred_element_type=F32) * pm_ref[nlev]
            for lv in range(nlev):
                xp = xs[lv][:, ps]
                amat = amat + lax.dot_general(xp, _pair_blocks(xp), _NT,
                                              preferred_element_type=F32) * pm_ref[lv]
            z128 = jnp.zeros((C_DIM, C_DIM), F32)
            st2 = jnp.concatenate(
                [jnp.concatenate([st_scr[2 * pr], z128], axis=1),
                 jnp.concatenate([z128, st_scr[2 * pr + 1]], axis=1)], axis=0).astype(BF16)
            outs.append(
                jnp.dot(amat.astype(BF16), _pair_blocks(vb[:, ps]), preferred_element_type=F32)
                + lax.dot_general(qt[:, ps], st2, _NT, preferred_element_type=F32))
        glast = gcum[length - 1:length, :]
        kdb = (kk * jnp.exp(glast - gcum)).astype(BF16)
        eg = jnp.exp(glast)
        for hd in range(C_HEADS):
            hsl = slice(hd * C_DIM, (hd + 1) * C_DIM)
            st_scr[hd] = st_scr[hd] * eg[:, hsl] + lax.dot_general(
                vb[:, hsl], kdb[:, hsl], _TN, preferred_element_type=F32)
        o = jnp.concatenate(outs, axis=1)
        inv = jnp.concatenate(
            [jnp.broadcast_to(
                lax.rsqrt(jnp.mean(jnp.square(o[:, hd * C_DIM:(hd + 1) * C_DIM]), axis=-1,
                                   keepdims=True) + EPS), (length, C_DIM))
             for hd in range(C_HEADS)], axis=1)
        gx = z_scr[rows, 3 * width:4 * width]
        og_ref[0, rows, :] = (o * inv * gn_ref[...] * (gx * jax.nn.sigmoid(gx))).astype(BF16)
        return carry

    lax.fori_loop(0, tm // length, chunk, 0)

    @pl.when(j == pl.num_programs(1) - 1)
    def _():
        for hd in range(C_HEADS):
            sout_ref[0, hd] = st_scr[hd].T


def _hgrn(x, g, w, lbp, s0, gn):
    b, t, d = x.shape
    width = w.shape[1] // 4
    assert width == C_HEADS * C_DIM
    tm = _pick_tm(t)
    length = min(CHUNK, t)
    nlev, sums, qmask, pmask = _hgrn_tables(length, width)
    tile = lambda c: pl.BlockSpec((1, tm, c), lambda i, j: (i, j, 0))
    state = pl.BlockSpec((1, C_HEADS, C_DIM, C_DIM), lambda i, j: (i, 0, 0, 0))
    return pl.pallas_call(
        functools.partial(_hgrn_kernel, tm=tm, length=length, nlev=nlev, width=width),
        grid=(b, t // tm),
        in_specs=[
            tile(d),
            _resident((1, d)),
            _resident(w.shape),
            _resident(lbp.shape),
            state,
            _resident((1, width)),
            _resident(sums.shape),
            _resident(qmask.shape),
            _resident(pmask.shape),
        ],
        out_specs=[tile(width), state],
        out_shape=[
            jax.ShapeDtypeStruct((b, t, width), BF16),
            jax.ShapeDtypeStruct((b, C_HEADS, C_DIM, C_DIM), F32),
        ],
        scratch_shapes=[
            pltpu.VMEM((tm, 4 * width), F32),
            pltpu.VMEM((C_HEADS, C_DIM, C_DIM), F32),
        ],
        compiler_params=_params(),
        name="inproj_hgrn",
    )(x, g, w, lbp, s0, jnp.tile(gn, (1, C_HEADS)), jnp.asarray(sums, BF16), jnp.asarray(qmask),
      jnp.asarray(pmask))


def _rel_bias_table(rel_bias):
    period = Q_ROWS + WIN
    offs = np.arange(period) - (Q_ROWS - 1)
    idx = np.clip(BAND_PAST - offs, -MAX_REL, MAX_REL) + MAX_REL
    u = rel_bias[:, idx].astype(F32)
    skew = jnp.tile(u, (1, Q_ROWS))[:, :Q_ROWS * (period - 1)].reshape(-1, Q_ROWS, period - 1)
    toep = skew[:, :, Q_ROWS - 1:Q_ROWS - 1 + WIN]
    first = (np.arange(Q_ROWS) // CHUNK * CHUNK)[:, None]
    col = np.arange(WIN)[None, :]
    return jnp.where((col >= first) & (col < first + BAND), toep, NEG)


def _stream(x, cache_k, cache_v, conv_hist, s0, p):
    b, t, d = x.shape
    bw = p["conv_w"].shape[-1]
    row = lambda a: a.reshape(1, -1)
    if cache_k is None:
        hist = jnp.zeros((b, HIST_PAD, bw), F32)
    else:
        hist = jnp.pad(conv_hist, ((0, 0), (HIST_OFF, 0), (0, 0)))
    q, k, v, kf, vf, co, ut = _inproj_ab(
        x, row(p["attn_norm"][0]), p["w_in_ab"], hist, p["conv_w"], row(p["conv_b"]),
        row(p["conv_ln_g"]), row(p["conv_ln_b"]))
    if cache_k is None:
        att = _attention(q, k, v, p["bias"], 0)
    else:
        w = cache_k.shape[1]
        assert w == BAND_PAST and t <= CHUNK
        pad = ((0, 0), (0, Q_ROWS - t), (0, 0))
        kp = jnp.concatenate([cache_k.reshape(b, w, A_WIDTH).astype(BF16), jnp.pad(k, pad)], axis=1)
        vp = jnp.concatenate([cache_v.reshape(b, w, A_WIDTH).astype(BF16), jnp.pad(v, pad)], axis=1)
        bias = jnp.where(np.arange(WIN)[None, None, :] < w + t, p["bias"], NEG)
        att = _attention(jnp.pad(q, pad), kp, vp, bias, w)[:, :t]
    y = _outproj_ffn(x, [(att, p["w_out_ab"][:A_WIDTH]), (co, p["w_out_ab"][A_WIDTH:])],
                     row(p["ffn_norm"][0]), p["ffn_w_in"][0], p["ffn_w_out"][0])
    keep = kf.shape[1]
    new_k = kf.reshape(1, b, keep, A_HEADS, A_HEAD_DIM)
    new_v = vf.reshape(1, b, keep, A_HEADS, A_HEAD_DIM)
    new_conv = ut[None, :, HIST_OFF:]
    if s0 is None:
        s0 = jnp.zeros((b, C_HEADS, C_DIM, C_DIM), F32)
    og, s_new = _hgrn(y, row(p["attn_norm"][1]), p["w_in_c"], p["lower_bounds"], s0,
                      row(p["c_norm_g"]))
    y = _outproj_ffn(y, [(og, p["w_out_c"])], row(p["ffn_norm"][1]), p["ffn_w_in"][1],
                     p["ffn_w_out"][1], fn=row(p["final_norm"]))
    return y, new_k, new_v, new_conv, s_new[None]


def kernel(x_prompt, x_sample, cache_a_k, cache_a_v, state_conv, state_hgrn, attn_norm, ffn_norm, final_norm, w_in_ab, rel_bias, conv_w, conv_b, conv_ln_g, conv_ln_b, w_out_ab, w_in_c, lower_bounds, c_norm_g, w_out_c, ffn_w_in, ffn_w_out):
    assert attn_norm.shape[0] == 2 and w_in_ab.shape[0] == 1 and w_in_c.shape[0] == 1
    p = dict(
        attn_norm=attn_norm, ffn_norm=ffn_norm, final_norm=final_norm,
        w_in_ab=w_in_ab[0].astype(BF16),
        bias=_rel_bias_table(rel_bias[0]),
        conv_w=jnp.pad(conv_w[0], ((0, HIST_PAD - CONV_WIDTH), (0, 0))),
        conv_b=conv_b[0], conv_ln_g=conv_ln_g[0], conv_ln_b=conv_ln_b[0],
        w_out_ab=w_out_ab[0].astype(BF16),
        w_in_c=w_in_c[0].astype(BF16),
        lower_bounds=lower_bounds,
        c_norm_g=c_norm_g[0],
        w_out_c=w_out_c[0].astype(BF16),
        ffn_w_in=ffn_w_in.astype(BF16),
        ffn_w_out=ffn_w_out.astype(BF16),
    )
    yp, akp, avp, cvp, hgp = _stream(x_prompt, None, None, None, None, p)
    ys, aks, avs, cvs, hgs = _stream(x_sample, cache_a_k[0], cache_a_v[0], state_conv[0],
                                     state_hgrn[0], p)
    return (yp, ys, akp, avp, cvp, hgp, aks, avs, cvs, hgs)
```

```python
import functools

import numpy as np
import jax
import jax.numpy as jnp
from jax import lax
from jax.experimental import pallas as pl
from jax.experimental.pallas import tpu as pltpu

EPS = 1e-6
NEG = -1e30
CHUNK = 64
BAND_PAST = 512
BAND = BAND_PAST + CHUNK
Q_ROWS = 4 * CHUNK
WIN = BAND_PAST + Q_ROWS
A_HEADS = 8
A_HEAD_DIM = 64
A_WIDTH = A_HEADS * A_HEAD_DIM
MAX_REL = 128
LANES = 128
SUBLANES = 8
BIAS_FREE = BAND_PAST - MAX_REL
CONV_WIDTH = 31
C_HEADS = 8
C_DIM = 128
HIST_ROWS = CONV_WIDTH - 1
HIST_PAD = 32
HIST_OFF = HIST_PAD - HIST_ROWS
CONV_ROWS = 32
FFN_ROWS = 512
VMEM_LIMIT_BYTES = 56 * 1024 * 1024
BF16 = jnp.bfloat16
F32 = jnp.float32

_NT = (((1,), (1,)), ((), ()))
_TN = (((0,), (0,)), ((), ()))


def _rms(x, g):
    return x * lax.rsqrt(jnp.mean(x * x, axis=-1, keepdims=True) + EPS) * g


def _resident(shape):
    n = len(shape)
    return pl.BlockSpec(shape, lambda *_: (0,) * n, pipeline_mode=pl.Buffered(1))


def _params():
    return pltpu.CompilerParams(
        dimension_semantics=("arbitrary", "arbitrary"),
        vmem_limit_bytes=VMEM_LIMIT_BYTES,
    )


def _pick_tm(t):
    for tm in (256, 128, 64):
        if t % tm == 0:
            return tm
    return t


def _inproj_ab_kernel(x_ref, g_ref, w_ref, hist_ref, cw_ref, cb_ref, lng_ref, lnb_ref,
                      q_ref, k_ref, v_ref, kf_ref, vf_ref, co_ref, ut_ref, uext_ref, ush_ref,
                      cacc_ref, *, tm, bw, rb):
    j = pl.program_id(1)
    h = _rms(x_ref[0], g_ref[...]).astype(BF16)
    zc = jnp.dot(h, w_ref[:, 3 * A_WIDTH:], preferred_element_type=F32)
    u = zc[:, :bw] * jax.nn.sigmoid(zc[:, bw:])

    @pl.when(j == 0)
    def _():
        uext_ref[0:HIST_PAD, :] = hist_ref[0]

    uext_ref[HIST_PAD:HIST_PAD + tm, :] = u
    n_sh = tm + HIST_PAD - SUBLANES
    for s in range(1, SUBLANES):
        ush_ref[s - 1] = uext_ref[pl.ds(s, n_sh), :]

    z = jnp.dot(h, w_ref[:, :3 * A_WIDTH], preferred_element_type=F32)
    q_ref[0] = (z[:, :A_WIDTH] * (A_HEAD_DIM ** -0.5)).astype(BF16)
    kk = z[:, A_WIDTH:2 * A_WIDTH]
    vv = z[:, 2 * A_WIDTH:]
    k_ref[0] = kk.astype(BF16)
    v_ref[0] = vv.astype(BF16)
    kf_ref[0] = kk
    vf_ref[0] = vv

    for r0 in range(0, tm, rb):
        acc = jnp.broadcast_to(cb_ref[...], (rb, bw))
        for t in range(CONV_WIDTH):
            a, s = divmod(t + HIST_OFF, SUBLANES)
            rows = pl.ds(r0 + a * SUBLANES, rb)
            src = uext_ref[rows, :] if s == 0 else ush_ref[s - 1, rows, :]
            acc = acc + cw_ref[t:t + 1, :] * src
        cacc_ref[r0:r0 + rb, :] = acc
    y = cacc_ref[...]
    mu = jnp.mean(y, axis=-1, keepdims=True)
    d = y - mu
    var = jnp.mean(d * d, axis=-1, keepdims=True)
    y = d * lax.rsqrt(var + EPS) * lng_ref[...] + lnb_ref[...]
    co_ref[0] = (y * jax.nn.sigmoid(y)).astype(BF16)
    tail = uext_ref[tm:tm + HIST_PAD, :]
    ut_ref[0] = tail
    uext_ref[0:HIST_PAD, :] = tail


def _inproj_ab(x, g, w, hist, cw, cb, lng, lnb):
    b, t, d = x.shape
    n = w.shape[1]
    bw = (n - 3 * A_WIDTH) // 2
    tm = _pick_tm(t)
    nt = t // tm
    keep = min(BAND_PAST, t)
    first_keep = nt - keep // tm
    rb = CONV_ROWS if tm % CONV_ROWS == 0 else tm
    tile = lambda c: pl.BlockSpec((1, tm, c), lambda i, j: (i, j, 0))
    keep_spec = pl.BlockSpec((1, tm, A_WIDTH), lambda i, j: (i, jnp.maximum(j - first_keep, 0), 0))
    return pl.pallas_call(
        functools.partial(_inproj_ab_kernel, tm=tm, bw=bw, rb=rb),
        grid=(b, nt),
        in_specs=[
            tile(d),
            _resident((1, d)),
            _resident((d, n)),
            pl.BlockSpec((1, HIST_PAD, bw), lambda i, j: (i, 0, 0)),
            _resident((HIST_PAD, bw)),
            _resident((1, bw)),
            _resident((1, bw)),
            _resident((1, bw)),
        ],
        out_specs=[
            tile(A_WIDTH), tile(A_WIDTH), tile(A_WIDTH),
            keep_spec, keep_spec,
            tile(bw),
            pl.BlockSpec((1, HIST_PAD, bw), lambda i, j: (i, 0, 0)),
        ],
        out_shape=[
            jax.ShapeDtypeStruct((b, t, A_WIDTH), BF16),
            jax.ShapeDtypeStruct((b, t, A_WIDTH), BF16),
            jax.ShapeDtypeStruct((b, t, A_WIDTH), BF16),
            jax.ShapeDtypeStruct((b, keep, A_WIDTH), F32),
            jax.ShapeDtypeStruct((b, keep, A_WIDTH), F32),
            jax.ShapeDtypeStruct((b, t, bw), BF16),
            jax.ShapeDtypeStruct((b, HIST_PAD, bw), F32),
        ],
        scratch_shapes=[
            pltpu.VMEM((tm + HIST_PAD, bw), F32),
            pltpu.VMEM((SUBLANES - 1, tm + HIST_PAD - SUBLANES, bw), F32),
            pltpu.VMEM((tm, bw), F32),
        ],
        compiler_params=_params(),
        name="inproj_ab",
    )(x, g, w, hist, cw, cb, lng, lnb)


def _attn_group(q_ref, k_ref, v_ref, b_ref, o_ref, kstart, wlen):
    q = q_ref[0]
    kw = k_ref[0, pl.ds(kstart, wlen), :]
    vw = v_ref[0, pl.ds(kstart, wlen), :]
    off = WIN - wlen
    low = lax.broadcasted_iota(jnp.int32, (1, LANES), 1) < A_HEAD_DIM
    head_lanes = (low.astype(BF16), (~low).astype(BF16))
    outs = []
    for h in range(A_HEADS):
        sl = slice(h // 2 * LANES, (h // 2 + 1) * LANES)
        s = lax.dot_general(q[:, sl] * head_lanes[h % 2], kw[:, sl], _NT,
                            preferred_element_type=F32)
        probs, sums = [], []
        for g in range(Q_ROWS // CHUNK):
            rows = slice(g * CHUNK, (g + 1) * CHUNK)
            lo = g * CHUNK
            tiles = []
            for c0 in range(off, WIN, LANES):
                c1 = c0 + LANES
                if c1 <= lo or c0 >= lo + BAND:
                    tiles.append(None)
                    continue
                st = s[rows, c0 - off:c1 - off]
                if c0 < lo or c1 > lo + BIAS_FREE:
                    st = st + b_ref[h, rows, c0:c1]
                tiles.append(st)
            live = [t for t in tiles if t is not None]
            m = jnp.max(functools.reduce(jnp.maximum, live), axis=-1, keepdims=True)
            ex = [None if t is None else jnp.exp(t - m) for t in tiles]
            sums.append(jnp.sum(functools.reduce(jnp.add, [e for e in ex if e is not None]),
                                axis=-1, keepdims=True))
            probs.append(jnp.concatenate(
                [jnp.zeros((CHUNK, LANES), BF16) if e is None else e.astype(BF16) for e in ex],
                axis=1))
        p = jnp.concatenate(probs, axis=0)
        o = jnp.dot(p, vw[:, sl], preferred_element_type=F32) / jnp.concatenate(sums, axis=0)
        if h % 2:
            outs.append(jnp.where(low, outs.pop(), o))
        else:
            outs.append(o)
    o_ref[0] = jnp.concatenate(outs, axis=-1).astype(BF16)


def _attn_kernel(q_ref, k_ref, v_ref, b_ref, o_ref, *, lead):
    g = pl.program_id(1)
    n_short = (BAND_PAST - lead) // Q_ROWS
    for i in range(n_short):
        @pl.when(g == i)
        def _(i=i):
            _attn_group(q_ref, k_ref, v_ref, b_ref, o_ref, 0, lead + (i + 1) * Q_ROWS)

    @pl.when(g >= n_short)
    def _():
        kstart = pl.multiple_of((g - n_short) * Q_ROWS, Q_ROWS)
        _attn_group(q_ref, k_ref, v_ref, b_ref, o_ref, kstart, WIN)


def _attention(q, k, v, bias, lead):
    b, tq, _ = q.shape
    tk = k.shape[1]
    assert tq % Q_ROWS == 0 and tk == lead + tq and (BAND_PAST - lead) % Q_ROWS == 0
    return pl.pallas_call(
        functools.partial(_attn_kernel, lead=lead),
        grid=(b, tq // Q_ROWS),
        in_specs=[
            pl.BlockSpec((1, Q_ROWS, A_WIDTH), lambda i, j: (i, j, 0)),
            pl.BlockSpec((1, tk, A_WIDTH), lambda i, j: (i, 0, 0)),
            pl.BlockSpec((1, tk, A_WIDTH), lambda i, j: (i, 0, 0)),
            _resident((A_HEADS, Q_ROWS, WIN)),
        ],
        out_specs=pl.BlockSpec((1, Q_ROWS, A_WIDTH), lambda i, j: (i, j, 0)),
        out_shape=jax.ShapeDtypeStruct((b, tq, A_WIDTH), BF16),
        compiler_params=_params(),
        name="band_attention",
    )(q, k, v, bias)


def _outproj_ffn_kernel(*refs, n_in, final):
    y_ref = refs[0]
    ins = refs[1:1 + 2 * n_in]
    fg_ref, w1_ref, w2_ref = refs[1 + 2 * n_in:4 + 2 * n_in]
    fn_ref = refs[4 + 2 * n_in] if final else None
    o_ref = refs[-1]
    y = y_ref[0]
    for i in range(n_in):
        y = y + jnp.dot(ins[2 * i][0], ins[2 * i + 1][...], preferred_element_type=F32)
    hn = _rms(y, fg_ref[...]).astype(BF16)
    hh = jnp.dot(hn, w1_ref[...], preferred_element_type=F32)
    dff = w2_ref.shape[0]
    a = hh[:, :dff]
    act = (a * jax.nn.sigmoid(a) * hh[:, dff:]).astype(BF16)
    y = y + jnp.dot(act, w2_ref[...], preferred_element_type=F32)
    if final:
        y = _rms(y, fn_ref[...])
    o_ref[0] = y


def _outproj_ffn(y, ins, fg, w1, w2, fn=None):
    b, t, d = y.shape
    tm = FFN_ROWS if t % FFN_ROWS == 0 else _pick_tm(t)
    tile = lambda c: pl.BlockSpec((1, tm, c), lambda i, j: (i, j, 0))
    in_specs = [tile(d)]
    args = [y]
    for a, w in ins:
        in_specs += [tile(a.shape[-1]), _resident(w.shape)]
        args += [a, w]
    in_specs += [_resident((1, d)), _resident(w1.shape), _resident(w2.shape)]
    args += [fg, w1, w2]
    if fn is not None:
        in_specs.append(_resident((1, d)))
        args.append(fn)
    return pl.pallas_call(
        functools.partial(_outproj_ffn_kernel, n_in=len(ins), final=fn is not None),
        grid=(b, t // tm),
        in_specs=in_specs,
        out_specs=tile(d),
        out_shape=jax.ShapeDtypeStruct((b, t, d), F32),
        compiler_params=_params(),
        name="outproj_ffn",
    )(*args)


def _hgrn_tables(length, width):
    nlev = int(np.log2(length))
    assert 2 ** nlev == length
    r = np.arange(length)
    sums = np.zeros((nlev + 1, length, length), np.float32)
    qmask = np.zeros((nlev, length, width), np.float32)
    pmask = np.zeros((nlev + 1, length, length), np.float32)
    for lv in range(nlev):
        hs = length >> (lv + 1)
        blk = r // (2 * hs)
        bnd = blk * 2 * hs + hs - 1
        is_q = (r % (2 * hs)) >= hs
        c = r[None, :]
        q_rows = (c > bnd[:, None]) & (c <= r[:, None])
        k_rows = (c > r[:, None]) & (c <= bnd[:, None])
        sums[lv] = np.where(is_q[:, None], q_rows, k_rows)
        qmask[lv] = is_q[:, None]
        pmask[lv] = (blk[:, None] == blk[None, :]) & is_q[:, None] & ~is_q[None, :]
    sums[nlev] = r[None, :] <= r[:, None]
    pmask[nlev] = np.eye(length)
    sums = sums.reshape((nlev + 1) * length, length)
    return (nlev, np.concatenate([sums, sums], axis=1), qmask,
            np.concatenate([pmask, pmask], axis=2))


def _pair_blocks(a):
    z = jnp.zeros((a.shape[0], C_DIM), a.dtype)
    return jnp.concatenate([jnp.concatenate([a[:, :C_DIM], z], axis=1),
                            jnp.concatenate([z, a[:, C_DIM:]], axis=1)], axis=0)


def _hgrn_kernel(x_ref, g_ref, w_ref, lbp_ref, s0_ref, gn_ref, sums_ref, qm_ref, pm_ref,
                 og_ref, sout_ref, z_scr, st_scr, *, tm, length, nlev, width):
    j = pl.program_id(1)
    h = _rms(x_ref[0], g_ref[...]).astype(BF16)
    z_scr[...] = jnp.dot(h, w_ref[...], preferred_element_type=F32)

    @pl.when(j == 0)
    def _():
        for hd in range(C_HEADS):
            st_scr[hd] = s0_ref[0, hd].T

    lbp = lbp_ref[...]
    mx = jnp.max(lbp, axis=0, keepdims=True)
    e = jnp.exp(lbp - mx)
    den = e[0:1] + e[1:2]
    s_first = e[0:1] / den
    lb = (s_first + e[1:2] / den) - s_first
    pair_w = 2 * C_DIM

    for r0 in range(0, tm, length):
        rows = slice(r0, r0 + length)
        q = z_scr[rows, 0:width]
        f = lb + (1.0 - lb) * jax.nn.sigmoid(z_scr[rows, width:2 * width])
        v = z_scr[rows, 2 * width:3 * width]
        logf = jnp.log(f)
        kk = 1.0 - f
        hi = logf.astype(BF16)
        mid = (logf - hi.astype(F32)).astype(BF16)
        dall = jnp.dot(sums_ref[...], jnp.concatenate([hi, mid], axis=0),
                       preferred_element_type=F32)
        gcum = dall[nlev * length:(nlev + 1) * length]
        qb = q.astype(BF16)
        kb = kk.astype(BF16)
        vb = v.astype(BF16)
        xs = []
        for lv in range(nlev):
            hs = length >> (lv + 1)
            ef = jnp.exp(dall[lv * length:(lv + 1) * length])
            if hs % 8 == 0:
                qk = jnp.concatenate(
                    [(q if i % 2 else kk)[i * hs:(i + 1) * hs] for i in range(length // hs)], axis=0)
            else:
                qk = jnp.where(qm_ref[lv] != 0.0, q, kk)
            xs.append((ef * qk).astype(BF16))
        qt = (q * jnp.exp(gcum)).astype(BF16)
        outs = []
        for pr in range(C_HEADS // 2):
            ps = slice(pr * pair_w, (pr + 1) * pair_w)
            amat = lax.dot_general(qb[:, ps], _pair_blocks(kb[:, ps]), _NT,
                                   preferred_element_type=F32) * pm_ref[nlev]
            for lv in range(nlev):
                xp = xs[lv][:, ps]
                amat = amat + lax.dot_general(xp, _pair_blocks(xp), _NT,
                                              preferred_element_type=F32) * pm_ref[lv]
            z128 = jnp.zeros((C_DIM, C_DIM), F32)
            st2 = jnp.concatenate(
                [jnp.concatenate([st_scr[2 * pr], z128], axis=1),
                 jnp.concatenate([z128, st_scr[2 * pr + 1]], axis=1)], axis=0).astype(BF16)
            outs.append(
                jnp.dot(amat.astype(BF16), _pair_blocks(vb[:, ps]), preferred_element_type=F32)
                + lax.dot_general(qt[:, ps], st2, _NT, preferred_element_type=F32))
        glast = gcum[length - 1:length, :]
        kdb = (kk * jnp.exp(glast - gcum)).astype(BF16)
        eg = jnp.exp(glast)
        for hd in range(C_HEADS):
            hsl = slice(hd * C_DIM, (hd + 1) * C_DIM)
            st_scr[hd] = st_scr[hd] * eg[:, hsl] + lax.dot_general(
                vb[:, hsl], kdb[:, hsl], _TN, preferred_element_type=F32)
        o = jnp.concatenate(outs, axis=1)
        inv = jnp.concatenate(
            [jnp.broadcast_to(
                lax.rsqrt(jnp.mean(jnp.square(o[:, hd * C_DIM:(hd + 1) * C_DIM]), axis=-1,
                                   keepdims=True) + EPS), (length, C_DIM))
             for hd in range(C_HEADS)], axis=1)
        gx = z_scr[rows, 3 * width:4 * width]
        og_ref[0, rows, :] = (o * inv * gn_ref[...] * (gx * jax.nn.sigmoid(gx))).astype(BF16)

    @pl.when(j == pl.num_programs(1) - 1)
    def _():
        for hd in range(C_HEADS):
            sout_ref[0, hd] = st_scr[hd].T


def _hgrn(x, g, w, lbp, s0, gn):
    b, t, d = x.shape
    width = w.shape[1] // 4
    assert width == C_HEADS * C_DIM
    tm = _pick_tm(t)
    length = min(CHUNK, t)
    nlev, sums, qmask, pmask = _hgrn_tables(length, width)
    tile = lambda c: pl.BlockSpec((1, tm, c), lambda i, j: (i, j, 0))
    state = pl.BlockSpec((1, C_HEADS, C_DIM, C_DIM), lambda i, j: (i, 0, 0, 0))
    return pl.pallas_call(
        functools.partial(_hgrn_kernel, tm=tm, length=length, nlev=nlev, width=width),
        grid=(b, t // tm),
        in_specs=[
            tile(d),
            _resident((1, d)),
            _resident(w.shape),
            _resident(lbp.shape),
            state,
            _resident((1, width)),
            _resident(sums.shape),
            _resident(qmask.shape),
            _resident(pmask.shape),
        ],
        out_specs=[tile(width), state],
        out_shape=[
            jax.ShapeDtypeStruct((b, t, width), BF16),
            jax.ShapeDtypeStruct((b, C_HEADS, C_DIM, C_DIM), F32),
        ],
        scratch_shapes=[
            pltpu.VMEM((tm, 4 * width), F32),
            pltpu.VMEM((C_HEADS, C_DIM, C_DIM), F32),
        ],
        compiler_params=_params(),
        name="inproj_hgrn",
    )(x, g, w, lbp, s0, jnp.tile(gn, (1, C_HEADS)), jnp.asarray(sums, BF16), jnp.asarray(qmask),
      jnp.asarray(pmask))


def _rel_bias_table(rel_bias):
    period = Q_ROWS + WIN
    offs = np.arange(period) - (Q_ROWS - 1)
    idx = np.clip(BAND_PAST - offs, -MAX_REL, MAX_REL) + MAX_REL
    u = rel_bias[:, idx].astype(F32)
    u = u - rel_bias[:, 2 * MAX_REL:].astype(F32)
    skew = jnp.tile(u, (1, Q_ROWS))[:, :Q_ROWS * (period - 1)].reshape(-1, Q_ROWS, period - 1)
    toep = skew[:, :, Q_ROWS - 1:Q_ROWS - 1 + WIN]
    first = (np.arange(Q_ROWS) // CHUNK * CHUNK)[:, None]
    col = np.arange(WIN)[None, :]
    return jnp.where((col >= first) & (col < first + BAND), toep, NEG)


def _stream(x, cache_k, cache_v, conv_hist, s0, p):
    b, t, d = x.shape
    bw = p["conv_w"].shape[-1]
    row = lambda a: a.reshape(1, -1)
    if cache_k is None:
        hist = jnp.zeros((b, HIST_PAD, bw), F32)
    else:
        hist = jnp.pad(conv_hist, ((0, 0), (HIST_OFF, 0), (0, 0)))
    q, k, v, kf, vf, co, ut = _inproj_ab(
        x, row(p["attn_norm"][0]), p["w_in_ab"], hist, p["conv_w"], row(p["conv_b"]),
        row(p["conv_ln_g"]), row(p["conv_ln_b"]))
    if cache_k is None:
        att = _attention(q, k, v, p["bias"], 0)
    else:
        w = cache_k.shape[1]
        assert w == BAND_PAST and t <= CHUNK
        pad = ((0, 0), (0, Q_ROWS - t), (0, 0))
        kp = jnp.concatenate([cache_k.reshape(b, w, A_WIDTH).astype(BF16), jnp.pad(k, pad)], axis=1)
        vp = jnp.concatenate([cache_v.reshape(b, w, A_WIDTH).astype(BF16), jnp.pad(v, pad)], axis=1)
        bias = jnp.where(np.arange(WIN)[None, None, :] < w + t, p["bias"], NEG)
        att = _attention(jnp.pad(q, pad), kp, vp, bias, w)[:, :t]
    y = _outproj_ffn(x, [(att, p["w_out_ab"][:A_WIDTH]), (co, p["w_out_ab"][A_WIDTH:])],
                     row(p["ffn_norm"][0]), p["ffn_w_in"][0], p["ffn_w_out"][0])
    keep = kf.shape[1]
    new_k = kf.reshape(1, b, keep, A_HEADS, A_HEAD_DIM)
    new_v = vf.reshape(1, b, keep, A_HEADS, A_HEAD_DIM)
    new_conv = ut[None, :, HIST_OFF:]
    if s0 is None:
        s0 = jnp.zeros((b, C_HEADS, C_DIM, C_DIM), F32)
    og, s_new = _hgrn(y, row(p["attn_norm"][1]), p["w_in_c"], p["lower_bounds"], s0,
                      row(p["c_norm_g"]))
    y = _outproj_ffn(y, [(og, p["w_out_c"])], row(p["ffn_norm"][1]), p["ffn_w_in"][1],
                     p["ffn_w_out"][1], fn=row(p["final_norm"]))
    return y, new_k, new_v, new_conv, s_new[None]


def kernel(x_prompt, x_sample, cache_a_k, cache_a_v, state_conv, state_hgrn, attn_norm, ffn_norm, final_norm, w_in_ab, rel_bias, conv_w, conv_b, conv_ln_g, conv_ln_b, w_out_ab, w_in_c, lower_bounds, c_norm_g, w_out_c, ffn_w_in, ffn_w_out):
    assert attn_norm.shape[0] == 2 and w_in_ab.shape[0] == 1 and w_in_c.shape[0] == 1
    p = dict(
        attn_norm=attn_norm, ffn_norm=ffn_norm, final_norm=final_norm,
        w_in_ab=w_in_ab[0].astype(BF16),
        bias=_rel_bias_table(rel_bias[0]),
        conv_w=jnp.pad(conv_w[0], ((0, HIST_PAD - CONV_WIDTH), (0, 0))),
        conv_b=conv_b[0], conv_ln_g=conv_ln_g[0], conv_ln_b=conv_ln_b[0],
        w_out_ab=w_out_ab[0].astype(BF16),
        w_in_c=w_in_c[0].astype(BF16),
        lower_bounds=lower_bounds,
        c_norm_g=c_norm_g[0],
        w_out_c=w_out_c[0].astype(BF16),
        ffn_w_in=ffn_w_in.astype(BF16),
        ffn_w_out=ffn_w_out.astype(BF16),
    )
    yp, akp, avp, cvp, hgp = _stream(x_prompt, None, None, None, None, p)
    ys, aks, avs, cvs, hgs = _stream(x_sample, cache_a_k[0], cache_a_v[0], state_conv[0],
                                     state_hgrn[0], p)
    return (yp, ys, akp, avp, cvp, hgp, aks, avs, cvs, hgs)
```

```python
import functools

import numpy as np
import jax
import jax.numpy as jnp
from jax import lax
from jax.experimental import pallas as pl
from jax.experimental.pallas import tpu as pltpu

EPS = 1e-6
NEG = -1e30
CHUNK = 64
BAND_PAST = 512
BAND = BAND_PAST + CHUNK
Q_ROWS = 4 * CHUNK
WIN = BAND_PAST + Q_ROWS
A_HEADS = 8
A_HEAD_DIM = 64
A_WIDTH = A_HEADS * A_HEAD_DIM
MAX_REL = 128
MXU_COLS = 256
LANES = 128
SUBLANES = 8
BIAS_FREE = BAND_PAST - MAX_REL
CONV_WIDTH = 31
C_HEADS = 8
C_DIM = 128
HIST_ROWS = CONV_WIDTH - 1
HIST_PAD = 32
HIST_OFF = HIST_PAD - HIST_ROWS
CONV_ROWS = 32
FFN_ROWS = 512
VMEM_LIMIT_BYTES = 56 * 1024 * 1024
BF16 = jnp.bfloat16
F32 = jnp.float32

_NT = (((1,), (1,)), ((), ()))
_TN = (((0,), (0,)), ((), ()))


def _rms(x, g):
    return x * lax.rsqrt(jnp.mean(x * x, axis=-1, keepdims=True) + EPS) * g


def _resident(shape):
    n = len(shape)
    return pl.BlockSpec(shape, lambda *_: (0,) * n, pipeline_mode=pl.Buffered(1))


def _params():
    return pltpu.CompilerParams(
        dimension_semantics=("arbitrary", "arbitrary"),
        vmem_limit_bytes=VMEM_LIMIT_BYTES,
    )


def _pick_tm(t):
    for tm in (256, 128, 64):
        if t % tm == 0:
            return tm
    return t


def _inproj_ab_kernel(x_ref, g_ref, w_ref, hist_ref, cw_ref, cb_ref, lng_ref, lnb_ref,
                      q_ref, k_ref, v_ref, kf_ref, vf_ref, co_ref, ut_ref, uext_ref, ush_ref,
                      cacc_ref, h_ref, *, tm, bw, rb):
    j = pl.program_id(1)
    h = _rms(x_ref[0], g_ref[...]).astype(BF16)
    zc = jnp.dot(h, w_ref[:, 3 * A_WIDTH:], preferred_element_type=F32)
    u = zc[:, :bw] * jax.nn.sigmoid(zc[:, bw:])

    @pl.when(j == 0)
    def _():
        uext_ref[0:HIST_PAD, :] = hist_ref[0]

    uext_ref[HIST_PAD:HIST_PAD + tm, :] = u
    n_sh = tm + HIST_PAD - SUBLANES
    for s in range(1, SUBLANES):
        ush_ref[s - 1] = uext_ref[pl.ds(s, n_sh), :]

    h_ref[...] = h

    def project(c0, c1):
        z = jnp.dot(h_ref[...], w_ref[:, c0:c1], preferred_element_type=F32)
        if c0 < A_WIDTH:
            q_ref[0, :, c0:c1] = (z * (A_HEAD_DIM ** -0.5)).astype(BF16)
        elif c0 < 2 * A_WIDTH:
            k_ref[0, :, c0 - A_WIDTH:c1 - A_WIDTH] = z.astype(BF16)
            kf_ref[0, :, c0 - A_WIDTH:c1 - A_WIDTH] = z
        else:
            v_ref[0, :, c0 - 2 * A_WIDTH:c1 - 2 * A_WIDTH] = z.astype(BF16)
            vf_ref[0, :, c0 - 2 * A_WIDTH:c1 - 2 * A_WIDTH] = z

    slices = list(range(0, 3 * A_WIDTH, MXU_COLS))

    def conv_block(i):
        if i < len(slices):
            project(slices[i], slices[i] + MXU_COLS)
        r0 = i * rb
        acc = jnp.broadcast_to(cb_ref[...], (rb, bw))
        for t in range(CONV_WIDTH):
            a, s = divmod(t + HIST_OFF, SUBLANES)
            rows = pl.ds(r0 + a * SUBLANES, rb)
            src = uext_ref[rows, :] if s == 0 else ush_ref[s - 1, rows, :]
            acc = acc + jnp.concatenate([cw_ref[t]] * (rb // SUBLANES), axis=0) * src
        cacc_ref[r0:r0 + rb, :] = acc

    for i in range(tm // rb):
        pl.when(j >= 0)(functools.partial(conv_block, i))
    for c0 in slices[tm // rb:]:
        project(c0, c0 + MXU_COLS)
    y = cacc_ref[...]
    mu = jnp.mean(y, axis=-1, keepdims=True)
    d = y - mu
    var = jnp.mean(d * d, axis=-1, keepdims=True)
    y = d * lax.rsqrt(var + EPS) * lng_ref[...] + lnb_ref[...]
    co_ref[0] = (y * jax.nn.sigmoid(y)).astype(BF16)
    tail = uext_ref[tm:tm + HIST_PAD, :]
    ut_ref[0] = tail
    uext_ref[0:HIST_PAD, :] = tail


def _inproj_ab(x, g, w, hist, cw, cb, lng, lnb):
    b, t, d = x.shape
    n = w.shape[1]
    bw = (n - 3 * A_WIDTH) // 2
    tm = _pick_tm(t)
    nt = t // tm
    keep = min(BAND_PAST, t)
    first_keep = nt - keep // tm
    rb = CONV_ROWS if tm % CONV_ROWS == 0 else tm
    tile = lambda c: pl.BlockSpec((1, tm, c), lambda i, j: (i, j, 0))
    keep_spec = pl.BlockSpec((1, tm, A_WIDTH), lambda i, j: (i, jnp.maximum(j - first_keep, 0), 0))
    return pl.pallas_call(
        functools.partial(_inproj_ab_kernel, tm=tm, bw=bw, rb=rb),
        grid=(b, nt),
        in_specs=[
            tile(d),
            _resident((1, d)),
            _resident((d, n)),
            pl.BlockSpec((1, HIST_PAD, bw), lambda i, j: (i, 0, 0)),
            _resident((CONV_WIDTH, SUBLANES, bw)),
            _resident((1, bw)),
            _resident((1, bw)),
            _resident((1, bw)),
        ],
        out_specs=[
            tile(A_WIDTH), tile(A_WIDTH), tile(A_WIDTH),
            keep_spec, keep_spec,
            tile(bw),
            pl.BlockSpec((1, HIST_PAD, bw), lambda i, j: (i, 0, 0)),
        ],
        out_shape=[
            jax.ShapeDtypeStruct((b, t, A_WIDTH), BF16),
            jax.ShapeDtypeStruct((b, t, A_WIDTH), BF16),
            jax.ShapeDtypeStruct((b, t, A_WIDTH), BF16),
            jax.ShapeDtypeStruct((b, keep, A_WIDTH), F32),
            jax.ShapeDtypeStruct((b, keep, A_WIDTH), F32),
            jax.ShapeDtypeStruct((b, t, bw), BF16),
            jax.ShapeDtypeStruct((b, HIST_PAD, bw), F32),
        ],
        scratch_shapes=[
            pltpu.VMEM((tm + HIST_PAD, bw), F32),
            pltpu.VMEM((SUBLANES - 1, tm + HIST_PAD - SUBLANES, bw), F32),
            pltpu.VMEM((tm, bw), F32),
            pltpu.VMEM((tm, d), BF16),
        ],
        compiler_params=_params(),
        name="inproj_ab",
    )(x, g, w, hist, cw, cb, lng, lnb)


def _attn_group(q_ref, k_ref, v_ref, b_ref, o_ref, kstart, wlen):
    q = q_ref[0]
    kw = k_ref[0, pl.ds(kstart, wlen), :]
    vw = v_ref[0, pl.ds(kstart, wlen), :]
    off = WIN - wlen
    low = lax.broadcasted_iota(jnp.int32, (1, LANES), 1) < A_HEAD_DIM
    head_lanes = (low.astype(BF16), (~low).astype(BF16))
    outs = []
    for h in range(A_HEADS):
        sl = slice(h // 2 * LANES, (h // 2 + 1) * LANES)
        s = lax.dot_general(q[:, sl] * head_lanes[h % 2], kw[:, sl], _NT,
                            preferred_element_type=F32)
        probs, sums = [], []
        for g in range(Q_ROWS // CHUNK):
            rows = slice(g * CHUNK, (g + 1) * CHUNK)
            lo = g * CHUNK
            tiles = []
            for c0 in range(off, WIN, LANES):
                c1 = c0 + LANES
                if c1 <= lo or c0 >= lo + BAND:
                    tiles.append(None)
                    continue
                st = s[rows, c0 - off:c1 - off]
                if c0 < lo or c1 > lo + BIAS_FREE:
                    st = st + b_ref[h, rows, c0:c1]
                tiles.append(st)
            live = [t for t in tiles if t is not None]
            m = jnp.max(functools.reduce(jnp.maximum, live), axis=-1, keepdims=True)
            ex = [None if t is None else jnp.exp(t - m) for t in tiles]
            sums.append(jnp.sum(functools.reduce(jnp.add, [e for e in ex if e is not None]),
                                axis=-1, keepdims=True))
            probs.append(jnp.concatenate(
                [jnp.zeros((CHUNK, LANES), BF16) if e is None else e.astype(BF16) for e in ex],
                axis=1))
        p = jnp.concatenate(probs, axis=0)
        o = jnp.dot(p, vw[:, sl], preferred_element_type=F32) / jnp.concatenate(sums, axis=0)
        if h % 2:
            outs.append(jnp.where(low, outs.pop(), o))
        else:
            outs.append(o)
    o_ref[0] = jnp.concatenate(outs, axis=-1).astype(BF16)


def _attn_kernel(q_ref, k_ref, v_ref, b_ref, o_ref, *, lead):
    g = pl.program_id(1)
    n_short = (BAND_PAST - lead) // Q_ROWS
    for i in range(n_short):
        @pl.when(g == i)
        def _(i=i):
            _attn_group(q_ref, k_ref, v_ref, b_ref, o_ref, 0, lead + (i + 1) * Q_ROWS)

    @pl.when(g >= n_short)
    def _():
        kstart = pl.multiple_of((g - n_short) * Q_ROWS, Q_ROWS)
        _attn_group(q_ref, k_ref, v_ref, b_ref, o_ref, kstart, WIN)


def _attention(q, k, v, bias, lead):
    b, tq, _ = q.shape
    tk = k.shape[1]
    assert tq % Q_ROWS == 0 and tk == lead + tq and (BAND_PAST - lead) % Q_ROWS == 0
    return pl.pallas_call(
        functools.partial(_attn_kernel, lead=lead),
        grid=(b, tq // Q_ROWS),
        in_specs=[
            pl.BlockSpec((1, Q_ROWS, A_WIDTH), lambda i, j: (i, j, 0)),
            pl.BlockSpec((1, tk, A_WIDTH), lambda i, j: (i, 0, 0)),
            pl.BlockSpec((1, tk, A_WIDTH), lambda i, j: (i, 0, 0)),
            _resident((A_HEADS, Q_ROWS, WIN)),
        ],
        out_specs=pl.BlockSpec((1, Q_ROWS, A_WIDTH), lambda i, j: (i, j, 0)),
        out_shape=jax.ShapeDtypeStruct((b, tq, A_WIDTH), BF16),
        compiler_params=_params(),
        name="band_attention",
    )(q, k, v, bias)


def _outproj_ffn_kernel(*refs, n_in, final):
    y_ref = refs[0]
    ins = refs[1:1 + 2 * n_in]
    fg_ref, w1_ref, w2_ref = refs[1 + 2 * n_in:4 + 2 * n_in]
    fn_ref = refs[4 + 2 * n_in] if final else None
    o_ref = refs[-1]
    y = y_ref[0]
    for i in range(n_in):
        y = y + jnp.dot(ins[2 * i][0], ins[2 * i + 1][...], preferred_element_type=F32)
    hn = _rms(y, fg_ref[...]).astype(BF16)
    hh = jnp.dot(hn, w1_ref[...], preferred_element_type=F32)
    dff = w2_ref.shape[0]
    a = hh[:, :dff]
    act = (a * jax.nn.sigmoid(a) * hh[:, dff:]).astype(BF16)
    y = y + jnp.dot(act, w2_ref[...], preferred_element_type=F32)
    if final:
        y = _rms(y, fn_ref[...])
    o_ref[0] = y


def _outproj_ffn(y, ins, fg, w1, w2, fn=None):
    b, t, d = y.shape
    tm = FFN_ROWS if t % FFN_ROWS == 0 else _pick_tm(t)
    tile = lambda c: pl.BlockSpec((1, tm, c), lambda i, j: (i, j, 0))
    in_specs = [tile(d)]
    args = [y]
    for a, w in ins:
        in_specs += [tile(a.shape[-1]), _resident(w.shape)]
        args += [a, w]
    in_specs += [_resident((1, d)), _resident(w1.shape), _resident(w2.shape)]
    args += [fg, w1, w2]
    if fn is not None:
        in_specs.append(_resident((1, d)))
        args.append(fn)
    return pl.pallas_call(
        functools.partial(_outproj_ffn_kernel, n_in=len(ins), final=fn is not None),
        grid=(b, t // tm),
        in_specs=in_specs,
        out_specs=tile(d),
        out_shape=jax.ShapeDtypeStruct((b, t, d), F32),
        compiler_params=_params(),
        name="outproj_ffn",
    )(*args)


def _hgrn_tables(length, width):
    nlev = int(np.log2(length))
    assert 2 ** nlev == length
    r = np.arange(length)
    sums = np.zeros((nlev + 1, length, length), np.float32)
    qmask = np.zeros((nlev, length, width), np.float32)
    pmask = np.zeros((nlev + 1, length, length), np.float32)
    for lv in range(nlev):
        hs = length >> (lv + 1)
        blk = r // (2 * hs)
        bnd = blk * 2 * hs + hs - 1
        is_q = (r % (2 * hs)) >= hs
        c = r[None, :]
        q_rows = (c > bnd[:, None]) & (c <= r[:, None])
        k_rows = (c > r[:, None]) & (c <= bnd[:, None])
        sums[lv] = np.where(is_q[:, None], q_rows, k_rows)
        qmask[lv] = is_q[:, None]
        pmask[lv] = (blk[:, None] == blk[None, :]) & is_q[:, None] & ~is_q[None, :]
    sums[nlev] = r[None, :] <= r[:, None]
    pmask[nlev] = np.eye(length)
    sums = sums.reshape((nlev + 1) * length, length)
    return (nlev, np.concatenate([sums, sums], axis=1), qmask,
            np.concatenate([pmask, pmask], axis=2))


def _pair_blocks(a):
    z = jnp.zeros((a.shape[0], C_DIM), a.dtype)
    return jnp.concatenate([jnp.concatenate([a[:, :C_DIM], z], axis=1),
                            jnp.concatenate([z, a[:, C_DIM:]], axis=1)], axis=0)


def _hgrn_kernel(x_ref, g_ref, w_ref, lbp_ref, s0_ref, gn_ref, sums_ref, qm_ref, pm_ref,
                 og_ref, sout_ref, z_scr, st_scr, *, tm, length, nlev, width):
    j = pl.program_id(1)
    h = _rms(x_ref[0], g_ref[...]).astype(BF16)
    z_scr[...] = jnp.dot(h, w_ref[...], preferred_element_type=F32)

    @pl.when(j == 0)
    def _():
        for hd in range(C_HEADS):
            st_scr[hd] = s0_ref[0, hd].T

    lbp = lbp_ref[...]
    mx = jnp.max(lbp, axis=0, keepdims=True)
    e = jnp.exp(lbp - mx)
    den = e[0:1] + e[1:2]
    s_first = e[0:1] / den
    lb = (s_first + e[1:2] / den) - s_first
    pair_w = 2 * C_DIM

    for r0 in range(0, tm, length):
        rows = slice(r0, r0 + length)
        q = z_scr[rows, 0:width]
        f = lb + (1.0 - lb) * jax.nn.sigmoid(z_scr[rows, width:2 * width])
        v = z_scr[rows, 2 * width:3 * width]
        logf = jnp.log(f)
        kk = 1.0 - f
        hi = logf.astype(BF16)
        mid = (logf - hi.astype(F32)).astype(BF16)
        dall = jnp.dot(sums_ref[...], jnp.concatenate([hi, mid], axis=0),
                       preferred_element_type=F32)
        gcum = dall[nlev * length:(nlev + 1) * length]
        qb = q.astype(BF16)
        kb = kk.astype(BF16)
        vb = v.astype(BF16)
        xs = []
        for lv in range(nlev):
            hs = length >> (lv + 1)
            ef = jnp.exp(dall[lv * length:(lv + 1) * length])
            if hs % 8 == 0:
                qk = jnp.concatenate(
                    [(q if i % 2 else kk)[i * hs:(i + 1) * hs] for i in range(length // hs)], axis=0)
            else:
                qk = jnp.where(qm_ref[lv] != 0.0, q, kk)
            xs.append((ef * qk).astype(BF16))
        qt = (q * jnp.exp(gcum)).astype(BF16)
        outs = []
        for pr in range(C_HEADS // 2):
            ps = slice(pr * pair_w, (pr + 1) * pair_w)
            amat = lax.dot_general(qb[:, ps], _pair_blocks(kb[:, ps]), _NT,
                                   preferred_element_type=F32) * pm_ref[nlev]
            for lv in range(nlev):
                xp = xs[lv][:, ps]
                amat = amat + lax.dot_general(xp, _pair_blocks(xp), _NT,
                                              preferred_element_type=F32) * pm_ref[lv]
            z128 = jnp.zeros((C_DIM, C_DIM), F32)
            st2 = jnp.concatenate(
                [jnp.concatenate([st_scr[2 * pr], z128], axis=1),
                 jnp.concatenate([z128, st_scr[2 * pr + 1]], axis=1)], axis=0).astype(BF16)
            outs.append(
                jnp.dot(amat.astype(BF16), _pair_blocks(vb[:, ps]), preferred_element_type=F32)
                + lax.dot_general(qt[:, ps], st2, _NT, preferred_element_type=F32))
        glast = gcum[length - 1:length, :]
        kdb = (kk * jnp.exp(glast - gcum)).astype(BF16)
        eg = jnp.exp(glast)
        for hd in range(C_HEADS):
            hsl = slice(hd * C_DIM, (hd + 1) * C_DIM)
            st_scr[hd] = st_scr[hd] * eg[:, hsl] + lax.dot_general(
                vb[:, hsl], kdb[:, hsl], _TN, preferred_element_type=F32)
        o = jnp.concatenate(outs, axis=1)
        inv = jnp.concatenate(
            [jnp.broadcast_to(
                lax.rsqrt(jnp.mean(jnp.square(o[:, hd * C_DIM:(hd + 1) * C_DIM]), axis=-1,
                                   keepdims=True) + EPS), (length, C_DIM))
             for hd in range(C_HEADS)], axis=1)
        gx = z_scr[rows, 3 * width:4 * width]
        og_ref[0, rows, :] = (o * inv * gn_ref[...] * (gx * jax.nn.sigmoid(gx))).astype(BF16)

    @pl.when(j == pl.num_programs(1) - 1)
    def _():
        for hd in range(C_HEADS):
            sout_ref[0, hd] = st_scr[hd].T


def _hgrn(x, g, w, lbp, s0, gn):
    b, t, d = x.shape
    width = w.shape[1] // 4
    assert width == C_HEADS * C_DIM
    tm = _pick_tm(t)
    length = min(CHUNK, t)
    nlev, sums, qmask, pmask = _hgrn_tables(length, width)
    tile = lambda c: pl.BlockSpec((1, tm, c), lambda i, j: (i, j, 0))
    state = pl.BlockSpec((1, C_HEADS, C_DIM, C_DIM), lambda i, j: (i, 0, 0, 0))
    return pl.pallas_call(
        functools.partial(_hgrn_kernel, tm=tm, length=length, nlev=nlev, width=width),
        grid=(b, t // tm),
        in_specs=[
            tile(d),
            _resident((1, d)),
            _resident(w.shape),
            _resident(lbp.shape),
            state,
            _resident((1, width)),
            _resident(sums.shape),
            _resident(qmask.shape),
            _resident(pmask.shape),
        ],
        out_specs=[tile(width), state],
        out_shape=[
            jax.ShapeDtypeStruct((b, t, width), BF16),
            jax.ShapeDtypeStruct((b, C_HEADS, C_DIM, C_DIM), F32),
        ],
        scratch_shapes=[
            pltpu.VMEM((tm, 4 * width), F32),
            pltpu.VMEM((C_HEADS, C_DIM, C_DIM), F32),
        ],
        compiler_params=_params(),
        name="inproj_hgrn",
    )(x, g, w, lbp, s0, jnp.tile(gn, (1, C_HEADS)), jnp.asarray(sums, BF16), jnp.asarray(qmask),
      jnp.asarray(pmask))


def _rel_bias_table(rel_bias):
    period = Q_ROWS + WIN
    offs = np.arange(period) - (Q_ROWS - 1)
    idx = np.clip(BAND_PAST - offs, -MAX_REL, MAX_REL) + MAX_REL
    u = rel_bias[:, idx].astype(F32)
    u = u - rel_bias[:, 2 * MAX_REL:].astype(F32)
    skew = jnp.tile(u, (1, Q_ROWS))[:, :Q_ROWS * (period - 1)].reshape(-1, Q_ROWS, period - 1)
    toep = skew[:, :, Q_ROWS - 1:Q_ROWS - 1 + WIN]
    first = (np.arange(Q_ROWS) // CHUNK * CHUNK)[:, None]
    col = np.arange(WIN)[None, :]
    return jnp.where((col >= first) & (col < first + BAND), toep, NEG)


def _stream(x, cache_k, cache_v, conv_hist, s0, p):
    b, t, d = x.shape
    bw = p["conv_w"].shape[-1]
    row = lambda a: a.reshape(1, -1)
    if cache_k is None:
        hist = jnp.zeros((b, HIST_PAD, bw), F32)
    else:
        hist = jnp.pad(conv_hist, ((0, 0), (HIST_OFF, 0), (0, 0)))
    q, k, v, kf, vf, co, ut = _inproj_ab(
        x, row(p["attn_norm"][0]), p["w_in_ab"], hist, p["conv_w"], row(p["conv_b"]),
        row(p["conv_ln_g"]), row(p["conv_ln_b"]))
    if cache_k is None:
        att = _attention(q, k, v, p["bias"], 0)
    else:
        w = cache_k.shape[1]
        assert w == BAND_PAST and t <= CHUNK
        pad = ((0, 0), (0, Q_ROWS - t), (0, 0))
        kp = jnp.concatenate([cache_k.reshape(b, w, A_WIDTH).astype(BF16), jnp.pad(k, pad)], axis=1)
        vp = jnp.concatenate([cache_v.reshape(b, w, A_WIDTH).astype(BF16), jnp.pad(v, pad)], axis=1)
        bias = jnp.where(np.arange(WIN)[None, None, :] < w + t, p["bias"], NEG)
        att = _attention(jnp.pad(q, pad), kp, vp, bias, w)[:, :t]
    y = _outproj_ffn(x, [(att, p["w_out_ab"][:A_WIDTH]), (co, p["w_out_ab"][A_WIDTH:])],
                     row(p["ffn_norm"][0]), p["ffn_w_in"][0], p["ffn_w_out"][0])
    keep = kf.shape[1]
    new_k = kf.reshape(1, b, keep, A_HEADS, A_HEAD_DIM)
    new_v = vf.reshape(1, b, keep, A_HEADS, A_HEAD_DIM)
    new_conv = ut[None, :, HIST_OFF:]
    if s0 is None:
        s0 = jnp.zeros((b, C_HEADS, C_DIM, C_DIM), F32)
    og, s_new = _hgrn(y, row(p["attn_norm"][1]), p["w_in_c"], p["lower_bounds"], s0,
                      row(p["c_norm_g"]))
    y = _outproj_ffn(y, [(og, p["w_out_c"])], row(p["ffn_norm"][1]), p["ffn_w_in"][1],
                     p["ffn_w_out"][1], fn=row(p["final_norm"]))
    return y, new_k, new_v, new_conv, s_new[None]


def kernel(x_prompt, x_sample, cache_a_k, cache_a_v, state_conv, state_hgrn, attn_norm, ffn_norm, final_norm, w_in_ab, rel_bias, conv_w, conv_b, conv_ln_g, conv_ln_b, w_out_ab, w_in_c, lower_bounds, c_norm_g, w_out_c, ffn_w_in, ffn_w_out):
    assert attn_norm.shape[0] == 2 and w_in_ab.shape[0] == 1 and w_in_c.shape[0] == 1
    p = dict(
        attn_norm=attn_norm, ffn_norm=ffn_norm, final_norm=final_norm,
        w_in_ab=w_in_ab[0].astype(BF16),
        bias=_rel_bias_table(rel_bias[0]),
        conv_w=jnp.broadcast_to(conv_w[0][:, None, :], (CONV_WIDTH, SUBLANES, conv_w.shape[-1])),
        conv_b=conv_b[0], conv_ln_g=conv_ln_g[0], conv_ln_b=conv_ln_b[0],
        w_out_ab=w_out_ab[0].astype(BF16),
        w_in_c=w_in_c[0].astype(BF16),
        lower_bounds=lower_bounds,
        c_norm_g=c_norm_g[0],
        w_out_c=w_out_c[0].astype(BF16),
        ffn_w_in=ffn_w_in.astype(BF16),
        ffn_w_out=ffn_w_out.astype(BF16),
    )
    yp, akp, avp, cvp, hgp = _stream(x_prompt, None, None, None, None, p)
    ys, aks, avs, cvs, hgs = _stream(x_sample, cache_a_k[0], cache_a_v[0], state_conv[0],
                                     state_hgrn[0], p)
    return (yp, ys, akp, avp, cvp, hgp, aks, avs, cvs, hgs)
```

```python
import functools

import numpy as np
import jax
import jax.numpy as jnp
from jax import lax
from jax.experimental import pallas as pl
from jax.experimental.pallas import tpu as pltpu

EPS = 1e-6
NEG = -1e30
CHUNK = 64
BAND_PAST = 512
BAND = BAND_PAST + CHUNK
Q_ROWS = 4 * CHUNK
WIN = BAND_PAST + Q_ROWS
A_HEADS = 8
A_HEAD_DIM = 64
A_WIDTH = A_HEADS * A_HEAD_DIM
MAX_REL = 128
MXU_COLS = 256
LANES = 128
SUBLANES = 8
BIAS_FREE = BAND_PAST - MAX_REL
CONV_WIDTH = 31
C_HEADS = 8
C_DIM = 128
HIST_ROWS = CONV_WIDTH - 1
HIST_PAD = 32
HIST_OFF = HIST_PAD - HIST_ROWS
CONV_ROWS = 32
FFN_ROWS = 512
HGRN_ROWS = 512
VMEM_LIMIT_BYTES = 56 * 1024 * 1024
BF16 = jnp.bfloat16
F32 = jnp.float32

_NT = (((1,), (1,)), ((), ()))
_TN = (((0,), (0,)), ((), ()))


def _rms(x, g):
    return x * lax.rsqrt(jnp.mean(x * x, axis=-1, keepdims=True) + EPS) * g


def _resident(shape):
    n = len(shape)
    return pl.BlockSpec(shape, lambda *_: (0,) * n, pipeline_mode=pl.Buffered(1))


def _params():
    return pltpu.CompilerParams(
        dimension_semantics=("arbitrary", "arbitrary"),
        vmem_limit_bytes=VMEM_LIMIT_BYTES,
    )


def _pick_tm(t):
    for tm in (256, 128, 64):
        if t % tm == 0:
            return tm
    return t


def _inproj_ab_kernel(x_ref, g_ref, w_ref, hist_ref, cw_ref, cb_ref, lng_ref, lnb_ref,
                      q_ref, k_ref, v_ref, kf_ref, vf_ref, co_ref, ut_ref, uext_ref,
                      cacc_ref, h_ref, *, tm, bw, rb):
    j = pl.program_id(1)
    h = _rms(x_ref[0], g_ref[...]).astype(BF16)
    zc = jnp.dot(h, w_ref[:, 3 * A_WIDTH:], preferred_element_type=F32)
    u = zc[:, :bw] * jax.nn.sigmoid(zc[:, bw:])

    @pl.when(j == 0)
    def _():
        uext_ref[0:HIST_PAD, :] = hist_ref[0]

    uext_ref[HIST_PAD:HIST_PAD + tm, :] = u
    h_ref[...] = h

    def project(c0, c1):
        z = jnp.dot(h_ref[...], w_ref[:, c0:c1], preferred_element_type=F32)
        if c0 < A_WIDTH:
            q_ref[0, :, c0:c1] = (z * (A_HEAD_DIM ** -0.5)).astype(BF16)
        elif c0 < 2 * A_WIDTH:
            k_ref[0, :, c0 - A_WIDTH:c1 - A_WIDTH] = z.astype(BF16)
            kf_ref[0, :, c0 - A_WIDTH:c1 - A_WIDTH] = z
        else:
            v_ref[0, :, c0 - 2 * A_WIDTH:c1 - 2 * A_WIDTH] = z.astype(BF16)
            vf_ref[0, :, c0 - 2 * A_WIDTH:c1 - 2 * A_WIDTH] = z

    slices = list(range(0, 3 * A_WIDTH, MXU_COLS))

    def conv_block(i):
        if i < len(slices):
            project(slices[i], slices[i] + MXU_COLS)
        r0 = i * rb
        for l0 in range(0, bw, LANES):
            lanes = slice(l0, l0 + LANES)
            x = uext_ref[r0:r0 + rb + HIST_PAD, lanes]
            acc = jnp.broadcast_to(cb_ref[:, lanes], (rb, LANES))
            for s in range(SUBLANES):
                xs = x if s == 0 else x[s:s + rb + HIST_PAD - SUBLANES]
                for a in range(HIST_PAD // SUBLANES + 1):
                    t = a * SUBLANES + s - HIST_OFF
                    if 0 <= t < CONV_WIDTH:
                        wt = jnp.concatenate([cw_ref[t, :, lanes]] * (rb // SUBLANES), axis=0)
                        acc = acc + wt * xs[a * SUBLANES:a * SUBLANES + rb]
            cacc_ref[r0:r0 + rb, lanes] = acc

    for i in range(tm // rb):
        pl.when(j >= 0)(functools.partial(conv_block, i))
    for c0 in slices[tm // rb:]:
        project(c0, c0 + MXU_COLS)
    y = cacc_ref[...]
    mu = jnp.mean(y, axis=-1, keepdims=True)
    d = y - mu
    var = jnp.mean(d * d, axis=-1, keepdims=True)
    y = d * lax.rsqrt(var + EPS) * lng_ref[...] + lnb_ref[...]
    co_ref[0] = (y * jax.nn.sigmoid(y)).astype(BF16)
    tail = uext_ref[tm:tm + HIST_PAD, :]
    ut_ref[0] = tail
    uext_ref[0:HIST_PAD, :] = tail


def _inproj_ab(x, g, w, hist, cw, cb, lng, lnb):
    b, t, d = x.shape
    n = w.shape[1]
    bw = (n - 3 * A_WIDTH) // 2
    tm = _pick_tm(t)
    nt = t // tm
    keep = min(BAND_PAST, t)
    first_keep = nt - keep // tm
    rb = CONV_ROWS if tm % CONV_ROWS == 0 else tm
    tile = lambda c: pl.BlockSpec((1, tm, c), lambda i, j: (i, j, 0))
    keep_spec = pl.BlockSpec((1, tm, A_WIDTH), lambda i, j: (i, jnp.maximum(j - first_keep, 0), 0))
    return pl.pallas_call(
        functools.partial(_inproj_ab_kernel, tm=tm, bw=bw, rb=rb),
        grid=(b, nt),
        in_specs=[
            tile(d),
            _resident((1, d)),
            _resident((d, n)),
            pl.BlockSpec((1, HIST_PAD, bw), lambda i, j: (i, 0, 0)),
            _resident((CONV_WIDTH, SUBLANES, bw)),
            _resident((1, bw)),
            _resident((1, bw)),
            _resident((1, bw)),
        ],
        out_specs=[
            tile(A_WIDTH), tile(A_WIDTH), tile(A_WIDTH),
            keep_spec, keep_spec,
            tile(bw),
            pl.BlockSpec((1, HIST_PAD, bw), lambda i, j: (i, 0, 0)),
        ],
        out_shape=[
            jax.ShapeDtypeStruct((b, t, A_WIDTH), BF16),
            jax.ShapeDtypeStruct((b, t, A_WIDTH), BF16),
            jax.ShapeDtypeStruct((b, t, A_WIDTH), BF16),
            jax.ShapeDtypeStruct((b, keep, A_WIDTH), F32),
            jax.ShapeDtypeStruct((b, keep, A_WIDTH), F32),
            jax.ShapeDtypeStruct((b, t, bw), BF16),
            jax.ShapeDtypeStruct((b, HIST_PAD, bw), F32),
        ],
        scratch_shapes=[
            pltpu.VMEM((tm + HIST_PAD, bw), F32),
            pltpu.VMEM((tm, bw), F32),
            pltpu.VMEM((tm, d), BF16),
        ],
        compiler_params=_params(),
        name="inproj_ab",
    )(x, g, w, hist, cw, cb, lng, lnb)


def _attn_group(q_ref, k_ref, v_ref, b_ref, o_ref, kstart, wlen):
    q = q_ref[0]
    kw = k_ref[0, pl.ds(kstart, wlen), :]
    vw = v_ref[0, pl.ds(kstart, wlen), :]
    off = WIN - wlen
    low = lax.broadcasted_iota(jnp.int32, (1, LANES), 1) < A_HEAD_DIM
    head_lanes = (low.astype(BF16), (~low).astype(BF16))
    outs = []
    for h in range(A_HEADS):
        sl = slice(h // 2 * LANES, (h // 2 + 1) * LANES)
        s = lax.dot_general(q[:, sl] * head_lanes[h % 2], kw[:, sl], _NT,
                            preferred_element_type=F32)
        probs, sums = [], []
        for g in range(Q_ROWS // CHUNK):
            rows = slice(g * CHUNK, (g + 1) * CHUNK)
            lo = g * CHUNK
            tiles = []
            for c0 in range(off, WIN, LANES):
                c1 = c0 + LANES
                if c1 <= lo or c0 >= lo + BAND:
                    tiles.append(None)
                    continue
                st = s[rows, c0 - off:c1 - off]
                if c0 < lo or c1 > lo + BIAS_FREE:
                    st = st + b_ref[h, rows, c0:c1]
                tiles.append(st)
            live = [t for t in tiles if t is not None]
            m = jnp.max(functools.reduce(jnp.maximum, live), axis=-1, keepdims=True)
            ex = [None if t is None else jnp.exp(t - m) for t in tiles]
            sums.append(jnp.sum(functools.reduce(jnp.add, [e for e in ex if e is not None]),
                                axis=-1, keepdims=True))
            probs.append(jnp.concatenate(
                [jnp.zeros((CHUNK, LANES), BF16) if e is None else e.astype(BF16) for e in ex],
                axis=1))
        p = jnp.concatenate(probs, axis=0)
        o = jnp.dot(p, vw[:, sl], preferred_element_type=F32) / jnp.concatenate(sums, axis=0)
        if h % 2:
            outs.append(jnp.where(low, outs.pop(), o))
        else:
            outs.append(o)
    o_ref[0] = jnp.concatenate(outs, axis=-1).astype(BF16)


def _attn_kernel(q_ref, k_ref, v_ref, b_ref, o_ref, *, lead):
    g = pl.program_id(1)
    n_short = (BAND_PAST - lead) // Q_ROWS
    for i in range(n_short):
        @pl.when(g == i)
        def _(i=i):
            _attn_group(q_ref, k_ref, v_ref, b_ref, o_ref, 0, lead + (i + 1) * Q_ROWS)

    @pl.when(g >= n_short)
    def _():
        kstart = pl.multiple_of((g - n_short) * Q_ROWS, Q_ROWS)
        _attn_group(q_ref, k_ref, v_ref, b_ref, o_ref, kstart, WIN)


def _attention(q, k, v, bias, lead):
    b, tq, _ = q.shape
    tk = k.shape[1]
    assert tq % Q_ROWS == 0 and tk == lead + tq and (BAND_PAST - lead) % Q_ROWS == 0
    return pl.pallas_call(
        functools.partial(_attn_kernel, lead=lead),
        grid=(b, tq // Q_ROWS),
        in_specs=[
            pl.BlockSpec((1, Q_ROWS, A_WIDTH), lambda i, j: (i, j, 0)),
            pl.BlockSpec((1, tk, A_WIDTH), lambda i, j: (i, 0, 0)),
            pl.BlockSpec((1, tk, A_WIDTH), lambda i, j: (i, 0, 0)),
            _resident((A_HEADS, Q_ROWS, WIN)),
        ],
        out_specs=pl.BlockSpec((1, Q_ROWS, A_WIDTH), lambda i, j: (i, j, 0)),
        out_shape=jax.ShapeDtypeStruct((b, tq, A_WIDTH), BF16),
        compiler_params=_params(),
        name="band_attention",
    )(q, k, v, bias)


def _outproj_ffn_kernel(*refs, n_in, final):
    y_ref = refs[0]
    ins = refs[1:1 + n_in]
    wo_ref, fg_ref, w1_ref, w2_ref = refs[1 + n_in:5 + n_in]
    fn_ref = refs[5 + n_in] if final else None
    o_ref = refs[-1]
    y = y_ref[0]
    r0 = 0
    for in_ref in ins:
        r1 = r0 + in_ref.shape[-1]
        y = y + jnp.dot(in_ref[0], wo_ref[r0:r1, :], preferred_element_type=F32)
        r0 = r1
    hn = _rms(y, fg_ref[...]).astype(BF16)
    hh = jnp.dot(hn, w1_ref[...], preferred_element_type=F32)
    dff = w2_ref.shape[0]
    a = hh[:, :dff]
    act = (a * jax.nn.sigmoid(a) * hh[:, dff:]).astype(BF16)
    y = y + jnp.dot(act, w2_ref[...], preferred_element_type=F32)
    if final:
        y = _rms(y, fn_ref[...])
    o_ref[0] = y


def _outproj_ffn(y, ins, wo, fg, w1, w2, fn=None):
    b, t, d = y.shape
    assert sum(a.shape[-1] for a in ins) == wo.shape[0]
    tm = FFN_ROWS if t % FFN_ROWS == 0 else _pick_tm(t)
    tile = lambda c: pl.BlockSpec((1, tm, c), lambda i, j: (i, j, 0))
    in_specs = [tile(d)] + [tile(a.shape[-1]) for a in ins]
    args = [y] + list(ins)
    in_specs += [_resident(wo.shape), _resident((1, d)), _resident(w1.shape), _resident(w2.shape)]
    args += [wo, fg, w1, w2]
    if fn is not None:
        in_specs.append(_resident((1, d)))
        args.append(fn)
    return pl.pallas_call(
        functools.partial(_outproj_ffn_kernel, n_in=len(ins), final=fn is not None),
        grid=(b, t // tm),
        in_specs=in_specs,
        out_specs=tile(d),
        out_shape=jax.ShapeDtypeStruct((b, t, d), F32),
        compiler_params=_params(),
        name="outproj_ffn",
    )(*args)


def _hgrn_tables(length, width):
    nlev = int(np.log2(length))
    assert 2 ** nlev == length
    r = np.arange(length)
    sums = np.zeros((nlev + 1, length, length), np.float32)
    qmask = np.zeros((nlev, length, width), np.float32)
    pmask = np.zeros((nlev + 1, length, length), np.float32)
    for lv in range(nlev):
        hs = length >> (lv + 1)
        blk = r // (2 * hs)
        bnd = blk * 2 * hs + hs - 1
        is_q = (r % (2 * hs)) >= hs
        c = r[None, :]
        q_rows = (c > bnd[:, None]) & (c <= r[:, None])
        k_rows = (c > r[:, None]) & (c <= bnd[:, None])
        sums[lv] = np.where(is_q[:, None], q_rows, k_rows)
        qmask[lv] = is_q[:, None]
        pmask[lv] = (blk[:, None] == blk[None, :]) & is_q[:, None] & ~is_q[None, :]
    sums[nlev] = r[None, :] <= r[:, None]
    pmask[nlev] = np.eye(length)
    sums = sums.reshape((nlev + 1) * length, length)
    return (nlev, np.concatenate([sums, sums], axis=1), qmask,
            np.concatenate([pmask, pmask], axis=2))


def _pair_blocks(a):
    z = jnp.zeros((a.shape[0], C_DIM), a.dtype)
    return jnp.concatenate([jnp.concatenate([a[:, :C_DIM], z], axis=1),
                            jnp.concatenate([z, a[:, C_DIM:]], axis=1)], axis=0)


def _hgrn_kernel(x_ref, g_ref, w_ref, lbp_ref, s0_ref, gn_ref, sums_ref, qm_ref, pm_ref,
                 og_ref, sout_ref, z_scr, st_scr, *, tm, length, nlev, width):
    j = pl.program_id(1)
    h = _rms(x_ref[0], g_ref[...]).astype(BF16)
    z_scr[...] = jnp.dot(h, w_ref[...], preferred_element_type=F32)

    @pl.when(j == 0)
    def _():
        for hd in range(C_HEADS):
            st_scr[hd] = s0_ref[0, hd].T

    lbp = lbp_ref[...]
    mx = jnp.max(lbp, axis=0, keepdims=True)
    e = jnp.exp(lbp - mx)
    den = e[0:1] + e[1:2]
    s_first = e[0:1] / den
    lb = (s_first + e[1:2] / den) - s_first
    pair_w = 2 * C_DIM

    for r0 in range(0, tm, length):
        rows = slice(r0, r0 + length)
        q = z_scr[rows, 0:width]
        f = lb + (1.0 - lb) * jax.nn.sigmoid(z_scr[rows, width:2 * width])
        v = z_scr[rows, 2 * width:3 * width]
        logf = jnp.log(f)
        kk = 1.0 - f
        hi = logf.astype(BF16)
        mid = (logf - hi.astype(F32)).astype(BF16)
        dall = jnp.dot(sums_ref[...], jnp.concatenate([hi, mid], axis=0),
                       preferred_element_type=F32)
        gcum = dall[nlev * length:(nlev + 1) * length]
        qb = q.astype(BF16)
        kb = kk.astype(BF16)
        vb = v.astype(BF16)
        xs = []
        for lv in range(nlev):
            hs = length >> (lv + 1)
            ef = jnp.exp(dall[lv * length:(lv + 1) * length])
            if hs % 8 == 0:
                qk = jnp.concatenate(
                    [(q if i % 2 else kk)[i * hs:(i + 1) * hs] for i in range(length // hs)], axis=0)
            else:
                qk = jnp.where(qm_ref[lv] != 0.0, q, kk)
            xs.append((ef * qk).astype(BF16))
        qt = (q * jnp.exp(gcum)).astype(BF16)
        outs = []
        for pr in range(C_HEADS // 2):
            ps = slice(pr * pair_w, (pr + 1) * pair_w)
            amat = lax.dot_general(qb[:, ps], _pair_blocks(kb[:, ps]), _NT,
                                   preferred_element_type=F32) * pm_ref[nlev]
            for lv in range(nlev):
                xp = xs[lv][:, ps]
                amat = amat + lax.dot_general(xp, _pair_blocks(xp), _NT,
                                              preferred_element_type=F32) * pm_ref[lv]
            z128 = jnp.zeros((C_DIM, C_DIM), F32)
            st2 = jnp.concatenate(
                [jnp.concatenate([st_scr[2 * pr], z128], axis=1),
                 jnp.concatenate([z128, st_scr[2 * pr + 1]], axis=1)], axis=0).astype(BF16)
            outs.append(
                jnp.dot(amat.astype(BF16), _pair_blocks(vb[:, ps]), preferred_element_type=F32)
                + lax.dot_general(qt[:, ps], st2, _NT, preferred_element_type=F32))
        glast = gcum[length - 1:length, :]
        kdb = (kk * jnp.exp(glast - gcum)).astype(BF16)
        eg = jnp.exp(glast)
        for hd in range(C_HEADS):
            hsl = slice(hd * C_DIM, (hd + 1) * C_DIM)
            st_scr[hd] = st_scr[hd] * eg[:, hsl] + lax.dot_general(
                vb[:, hsl], kdb[:, hsl], _TN, preferred_element_type=F32)
        o = jnp.concatenate(outs, axis=1)
        inv = jnp.concatenate(
            [jnp.broadcast_to(
                lax.rsqrt(jnp.mean(jnp.square(o[:, hd * C_DIM:(hd + 1) * C_DIM]), axis=-1,
                                   keepdims=True) + EPS), (length, C_DIM))
             for hd in range(C_HEADS)], axis=1)
        gx = z_scr[rows, 3 * width:4 * width]
        og_ref[0, rows, :] = (o * inv * gn_ref[...] * (gx * jax.nn.sigmoid(gx))).astype(BF16)

    @pl.when(j == pl.num_programs(1) - 1)
    def _():
        for hd in range(C_HEADS):
            sout_ref[0, hd] = st_scr[hd].T


def _hgrn(x, g, w, lbp, s0, gn):
    b, t, d = x.shape
    width = w.shape[1] // 4
    assert width == C_HEADS * C_DIM
    tm = HGRN_ROWS if t % HGRN_ROWS == 0 else _pick_tm(t)
    length = min(CHUNK, t)
    nlev, sums, qmask, pmask = _hgrn_tables(length, width)
    tile = lambda c: pl.BlockSpec((1, tm, c), lambda i, j: (i, j, 0))
    state = pl.BlockSpec((1, C_HEADS, C_DIM, C_DIM), lambda i, j: (i, 0, 0, 0))
    return pl.pallas_call(
        functools.partial(_hgrn_kernel, tm=tm, length=length, nlev=nlev, width=width),
        grid=(b, t // tm),
        in_specs=[
            tile(d),
            _resident((1, d)),
            _resident(w.shape),
            _resident(lbp.shape),
            state,
            _resident((1, width)),
            _resident(sums.shape),
            _resident(qmask.shape),
            _resident(pmask.shape),
        ],
        out_specs=[tile(width), state],
        out_shape=[
            jax.ShapeDtypeStruct((b, t, width), BF16),
            jax.ShapeDtypeStruct((b, C_HEADS, C_DIM, C_DIM), F32),
        ],
        scratch_shapes=[
            pltpu.VMEM((tm, 4 * width), F32),
            pltpu.VMEM((C_HEADS, C_DIM, C_DIM), F32),
        ],
        compiler_params=_params(),
        name="inproj_hgrn",
    )(x, g, w, lbp, s0, jnp.tile(gn, (1, C_HEADS)), jnp.asarray(sums, BF16), jnp.asarray(qmask),
      jnp.asarray(pmask))


def _rel_bias_table(rel_bias):
    period = Q_ROWS + WIN
    offs = np.arange(period) - (Q_ROWS - 1)
    idx = np.clip(BAND_PAST - offs, -MAX_REL, MAX_REL) + MAX_REL
    u = rel_bias[:, idx].astype(F32)
    u = u - rel_bias[:, 2 * MAX_REL:].astype(F32)
    skew = jnp.tile(u, (1, Q_ROWS))[:, :Q_ROWS * (period - 1)].reshape(-1, Q_ROWS, period - 1)
    toep = skew[:, :, Q_ROWS - 1:Q_ROWS - 1 + WIN]
    first = (np.arange(Q_ROWS) // CHUNK * CHUNK)[:, None]
    col = np.arange(WIN)[None, :]
    return jnp.where((col >= first) & (col < first + BAND), toep, NEG)


def _stream(x, cache_k, cache_v, conv_hist, s0, p):
    b, t, d = x.shape
    bw = p["conv_w"].shape[-1]
    row = lambda a: a.reshape(1, -1)
    if cache_k is None:
        hist = jnp.zeros((b, HIST_PAD, bw), F32)
    else:
        hist = jnp.pad(conv_hist, ((0, 0), (HIST_OFF, 0), (0, 0)))
    q, k, v, kf, vf, co, ut = _inproj_ab(
        x, row(p["attn_norm"][0]), p["w_in_ab"], hist, p["conv_w"], row(p["conv_b"]),
        row(p["conv_ln_g"]), row(p["conv_ln_b"]))
    if cache_k is None:
        att = _attention(q, k, v, p["bias"], 0)
    else:
        w = cache_k.shape[1]
        assert w == BAND_PAST and t <= CHUNK
        pad = ((0, 0), (0, Q_ROWS - t), (0, 0))
        kp = jnp.concatenate([cache_k.reshape(b, w, A_WIDTH).astype(BF16), jnp.pad(k, pad)], axis=1)
        vp = jnp.concatenate([cache_v.reshape(b, w, A_WIDTH).astype(BF16), jnp.pad(v, pad)], axis=1)
        bias = jnp.where(np.arange(WIN)[None, None, :] < w + t, p["bias"], NEG)
        att = _attention(jnp.pad(q, pad), kp, vp, bias, w)[:, :t]
    y = _outproj_ffn(x, [att, co], p["w_out_ab"], row(p["ffn_norm"][0]), p["ffn_w_in"][0],
                     p["ffn_w_out"][0])
    keep = kf.shape[1]
    new_k = kf.reshape(1, b, keep, A_HEADS, A_HEAD_DIM)
    new_v = vf.reshape(1, b, keep, A_HEADS, A_HEAD_DIM)
    new_conv = ut[None, :, HIST_OFF:]
    if s0 is None:
        s0 = jnp.zeros((b, C_HEADS, C_DIM, C_DIM), F32)
    og, s_new = _hgrn(y, row(p["attn_norm"][1]), p["w_in_c"], p["lower_bounds"], s0,
                      row(p["c_norm_g"]))
    y = _outproj_ffn(y, [og], p["w_out_c"], row(p["ffn_norm"][1]), p["ffn_w_in"][1],
                     p["ffn_w_out"][1], fn=row(p["final_norm"]))
    return y, new_k, new_v, new_conv, s_new[None]


def kernel(x_prompt, x_sample, cache_a_k, cache_a_v, state_conv, state_hgrn, attn_norm, ffn_norm, final_norm, w_in_ab, rel_bias, conv_w, conv_b, conv_ln_g, conv_ln_b, w_out_ab, w_in_c, lower_bounds, c_norm_g, w_out_c, ffn_w_in, ffn_w_out):
    assert attn_norm.shape[0] == 2 and w_in_ab.shape[0] == 1 and w_in_c.shape[0] == 1
    p = dict(
        attn_norm=attn_norm, ffn_norm=ffn_norm, final_norm=final_norm,
        w_in_ab=w_in_ab[0].astype(BF16),
        bias=_rel_bias_table(rel_bias[0]),
        conv_w=jnp.broadcast_to(conv_w[0][:, None, :], (CONV_WIDTH, SUBLANES, conv_w.shape[-1])),
        conv_b=conv_b[0], conv_ln_g=conv_ln_g[0], conv_ln_b=conv_ln_b[0],
        w_out_ab=w_out_ab[0].astype(BF16),
        w_in_c=w_in_c[0].astype(BF16),
        lower_bounds=lower_bounds,
        c_norm_g=c_norm_g[0],
        w_out_c=w_out_c[0].astype(BF16),
        ffn_w_in=ffn_w_in.astype(BF16),
        ffn_w_out=ffn_w_out.astype(BF16),
    )
    yp, akp, avp, cvp, hgp = _stream(x_prompt, None, None, None, None, p)
    ys, aks, avs, cvs, hgs = _stream(x_sample, cache_a_k[0], cache_a_v[0], state_conv[0],
                                     state_hgrn[0], p)
    return (yp, ys, akp, avp, cvp, hgp, aks, avs, cvs, hgs)
```

```python
import functools

import numpy as np
import jax
import jax.numpy as jnp
from jax import lax
from jax.experimental import pallas as pl
from jax.experimental.pallas import tpu as pltpu

EPS = 1e-6
NEG = -1e30
CHUNK = 64
BAND_PAST = 512
BAND = BAND_PAST + CHUNK
Q_ROWS = 4 * CHUNK
WIN = BAND_PAST + Q_ROWS
A_HEADS = 8
A_HEAD_DIM = 64
A_WIDTH = A_HEADS * A_HEAD_DIM
MAX_REL = 128
MXU_COLS = 256
LANES = 128
SUBLANES = 8
BIAS_FREE = BAND_PAST - MAX_REL
CONV_WIDTH = 31
C_HEADS = 8
C_DIM = 128
HIST_ROWS = CONV_WIDTH - 1
HIST_PAD = 32
HIST_OFF = HIST_PAD - HIST_ROWS
CONV_ROWS = 32
FFN_ROWS = 512
HGRN_ROWS = 512
INPROJ_ROWS = 512
VMEM_LIMIT_BYTES = 56 * 1024 * 1024
BF16 = jnp.bfloat16
F32 = jnp.float32

_NT = (((1,), (1,)), ((), ()))
_TN = (((0,), (0,)), ((), ()))


def _rms(x, g):
    return x * lax.rsqrt(jnp.mean(x * x, axis=-1, keepdims=True) + EPS) * g


def _resident(shape):
    n = len(shape)
    return pl.BlockSpec(shape, lambda *_: (0,) * n, pipeline_mode=pl.Buffered(1))


def _params():
    return pltpu.CompilerParams(
        dimension_semantics=("arbitrary", "arbitrary"),
        vmem_limit_bytes=VMEM_LIMIT_BYTES,
    )


def _pick_tm(t):
    for tm in (256, 128, 64):
        if t % tm == 0:
            return tm
    return t


def _inproj_ab_kernel(x_ref, g_ref, w_ref, hist_ref, cw_ref, cb_ref, lng_ref, lnb_ref,
                      q_ref, k_ref, v_ref, kf_ref, vf_ref, co_ref, ut_ref, uext_ref,
                      cacc_ref, h_ref, *, tm, bw, rb):
    j = pl.program_id(1)
    h = _rms(x_ref[0], g_ref[...]).astype(BF16)
    zc = jnp.dot(h, w_ref[:, 3 * A_WIDTH:], preferred_element_type=F32)
    u = zc[:, :bw] * jax.nn.sigmoid(zc[:, bw:])

    @pl.when(j == 0)
    def _():
        uext_ref[0:HIST_PAD, :] = hist_ref[0]

    uext_ref[HIST_PAD:HIST_PAD + tm, :] = u
    h_ref[...] = h

    def project(c0, c1):
        z = jnp.dot(h_ref[...], w_ref[:, c0:c1], preferred_element_type=F32)
        if c0 < A_WIDTH:
            q_ref[0, :, c0:c1] = (z * (A_HEAD_DIM ** -0.5)).astype(BF16)
        elif c0 < 2 * A_WIDTH:
            k_ref[0, :, c0 - A_WIDTH:c1 - A_WIDTH] = z.astype(BF16)
            kf_ref[0, :, c0 - A_WIDTH:c1 - A_WIDTH] = z
        else:
            v_ref[0, :, c0 - 2 * A_WIDTH:c1 - 2 * A_WIDTH] = z.astype(BF16)
            vf_ref[0, :, c0 - 2 * A_WIDTH:c1 - 2 * A_WIDTH] = z

    slices = list(range(0, 3 * A_WIDTH, MXU_COLS))

    def conv_block(i):
        if i < len(slices):
            project(slices[i], slices[i] + MXU_COLS)
        r0 = i * rb
        for l0 in range(0, bw, LANES):
            lanes = slice(l0, l0 + LANES)
            x = uext_ref[r0:r0 + rb + HIST_PAD, lanes]
            acc = jnp.broadcast_to(cb_ref[:, lanes], (rb, LANES))
            for s in range(SUBLANES):
                xs = x if s == 0 else x[s:s + rb + HIST_PAD - SUBLANES]
                for a in range(HIST_PAD // SUBLANES + 1):
                    t = a * SUBLANES + s - HIST_OFF
                    if 0 <= t < CONV_WIDTH:
                        wt = jnp.concatenate([cw_ref[t, :, lanes]] * (rb // SUBLANES), axis=0)
                        acc = acc + wt * xs[a * SUBLANES:a * SUBLANES + rb]
            cacc_ref[r0:r0 + rb, lanes] = acc

    for i in range(tm // rb):
        pl.when(j >= 0)(functools.partial(conv_block, i))
    for c0 in slices[tm // rb:]:
        project(c0, c0 + MXU_COLS)
    y = cacc_ref[...]
    mu = jnp.mean(y, axis=-1, keepdims=True)
    d = y - mu
    var = jnp.mean(d * d, axis=-1, keepdims=True)
    y = d * lax.rsqrt(var + EPS) * lng_ref[...] + lnb_ref[...]
    co_ref[0] = (y * jax.nn.sigmoid(y)).astype(BF16)
    tail = uext_ref[tm:tm + HIST_PAD, :]
    ut_ref[0] = tail
    uext_ref[0:HIST_PAD, :] = tail


def _inproj_ab(x, g, w, hist, cw, cb, lng, lnb):
    b, t, d = x.shape
    n = w.shape[1]
    bw = (n - 3 * A_WIDTH) // 2
    tm = INPROJ_ROWS if t % INPROJ_ROWS == 0 else _pick_tm(t)
    nt = t // tm
    keep = min(BAND_PAST, t)
    first_keep = nt - keep // tm
    rb = CONV_ROWS if tm % CONV_ROWS == 0 else tm
    tile = lambda c: pl.BlockSpec((1, tm, c), lambda i, j: (i, j, 0))
    keep_spec = pl.BlockSpec((1, tm, A_WIDTH), lambda i, j: (i, jnp.maximum(j - first_keep, 0), 0))
    return pl.pallas_call(
        functools.partial(_inproj_ab_kernel, tm=tm, bw=bw, rb=rb),
        grid=(b, nt),
        in_specs=[
            tile(d),
            _resident((1, d)),
            _resident((d, n)),
            pl.BlockSpec((1, HIST_PAD, bw), lambda i, j: (i, 0, 0)),
            _resident((CONV_WIDTH, SUBLANES, bw)),
            _resident((1, bw)),
            _resident((1, bw)),
            _resident((1, bw)),
        ],
        out_specs=[
            tile(A_WIDTH), tile(A_WIDTH), tile(A_WIDTH),
            keep_spec, keep_spec,
            tile(bw),
            pl.BlockSpec((1, HIST_PAD, bw), lambda i, j: (i, 0, 0)),
        ],
        out_shape=[
            jax.ShapeDtypeStruct((b, t, A_WIDTH), BF16),
            jax.ShapeDtypeStruct((b, t, A_WIDTH), BF16),
            jax.ShapeDtypeStruct((b, t, A_WIDTH), BF16),
            jax.ShapeDtypeStruct((b, keep, A_WIDTH), F32),
            jax.ShapeDtypeStruct((b, keep, A_WIDTH), F32),
            jax.ShapeDtypeStruct((b, t, bw), BF16),
            jax.ShapeDtypeStruct((b, HIST_PAD, bw), F32),
        ],
        scratch_shapes=[
            pltpu.VMEM((tm + HIST_PAD, bw), F32),
            pltpu.VMEM((tm, bw), F32),
            pltpu.VMEM((tm, d), BF16),
        ],
        compiler_params=_params(),
        name="inproj_ab",
    )(x, g, w, hist, cw, cb, lng, lnb)


def _attn_group(q_ref, k_ref, v_ref, b_ref, o_ref, kstart, wlen):
    q = q_ref[0]
    kw = k_ref[0, pl.ds(kstart, wlen), :]
    vw = v_ref[0, pl.ds(kstart, wlen), :]
    off = WIN - wlen
    low = lax.broadcasted_iota(jnp.int32, (1, LANES), 1) < A_HEAD_DIM
    head_lanes = (low.astype(BF16), (~low).astype(BF16))
    ones = jnp.ones((wlen, LANES), BF16)
    outs = []
    for h in range(A_HEADS):
        sl = slice(h // 2 * LANES, (h // 2 + 1) * LANES)
        s = lax.dot_general(q[:, sl] * head_lanes[h % 2], kw[:, sl], _NT,
                            preferred_element_type=F32)
        probs = []
        for g in range(Q_ROWS // CHUNK):
            rows = slice(g * CHUNK, (g + 1) * CHUNK)
            lo = g * CHUNK
            tiles = []
            for c0 in range(off, WIN, LANES):
                c1 = c0 + LANES
                if c1 <= lo or c0 >= lo + BAND:
                    tiles.append(None)
                    continue
                st = s[rows, c0 - off:c1 - off]
                if c0 < lo or c1 > lo + BIAS_FREE:
                    st = st + b_ref[h, rows, c0:c1]
                tiles.append(st)
            live = [t for t in tiles if t is not None]
            m = jnp.max(functools.reduce(jnp.maximum, live), axis=-1, keepdims=True)
            probs.append(jnp.concatenate(
                [jnp.zeros((CHUNK, LANES), BF16) if t is None else jnp.exp(t - m).astype(BF16)
                 for t in tiles], axis=1))
        p = jnp.concatenate(probs, axis=0)
        pv = jnp.dot(p, jnp.concatenate([vw[:, sl], ones], axis=1), preferred_element_type=F32)
        o = pv[:, :LANES] / pv[:, LANES:]
        if h % 2:
            outs.append(jnp.where(low, outs.pop(), o))
        else:
            outs.append(o)
    o_ref[0] = jnp.concatenate(outs, axis=-1).astype(BF16)


def _attn_kernel(q_ref, k_ref, v_ref, b_ref, o_ref, *, lead):
    g = pl.program_id(1)
    n_short = (BAND_PAST - lead) // Q_ROWS
    for i in range(n_short):
        @pl.when(g == i)
        def _(i=i):
            _attn_group(q_ref, k_ref, v_ref, b_ref, o_ref, 0, lead + (i + 1) * Q_ROWS)

    @pl.when(g >= n_short)
    def _():
        kstart = pl.multiple_of((g - n_short) * Q_ROWS, Q_ROWS)
        _attn_group(q_ref, k_ref, v_ref, b_ref, o_ref, kstart, WIN)


def _attention(q, k, v, bias, lead):
    b, tq, _ = q.shape
    tk = k.shape[1]
    assert tq % Q_ROWS == 0 and tk == lead + tq and (BAND_PAST - lead) % Q_ROWS == 0
    return pl.pallas_call(
        functools.partial(_attn_kernel, lead=lead),
        grid=(b, tq // Q_ROWS),
        in_specs=[
            pl.BlockSpec((1, Q_ROWS, A_WIDTH), lambda i, j: (i, j, 0)),
            pl.BlockSpec((1, tk, A_WIDTH), lambda i, j: (i, 0, 0)),
            pl.BlockSpec((1, tk, A_WIDTH), lambda i, j: (i, 0, 0)),
            _resident((A_HEADS, Q_ROWS, WIN)),
        ],
        out_specs=pl.BlockSpec((1, Q_ROWS, A_WIDTH), lambda i, j: (i, j, 0)),
        out_shape=jax.ShapeDtypeStruct((b, tq, A_WIDTH), BF16),
        compiler_params=_params(),
        name="band_attention",
    )(q, k, v, bias)


def _outproj_ffn_kernel(*refs, n_in, final):
    y_ref = refs[0]
    ins = refs[1:1 + n_in]
    wo_ref, fg_ref, w1_ref, w2_ref = refs[1 + n_in:5 + n_in]
    fn_ref = refs[5 + n_in] if final else None
    o_ref = refs[-1]
    y = y_ref[0]
    r0 = 0
    for in_ref in ins:
        r1 = r0 + in_ref.shape[-1]
        y = y + jnp.dot(in_ref[0], wo_ref[r0:r1, :], preferred_element_type=F32)
        r0 = r1
    hn = _rms(y, fg_ref[...]).astype(BF16)
    hh = jnp.dot(hn, w1_ref[...], preferred_element_type=F32)
    dff = w2_ref.shape[0]
    a = hh[:, :dff]
    act = (a * jax.nn.sigmoid(a) * hh[:, dff:]).astype(BF16)
    y = y + jnp.dot(act, w2_ref[...], preferred_element_type=F32)
    if final:
        y = _rms(y, fn_ref[...])
    o_ref[0] = y


def _outproj_ffn(y, ins, wo, fg, w1, w2, fn=None):
    b, t, d = y.shape
    assert sum(a.shape[-1] for a in ins) == wo.shape[0]
    tm = FFN_ROWS if t % FFN_ROWS == 0 else _pick_tm(t)
    tile = lambda c: pl.BlockSpec((1, tm, c), lambda i, j: (i, j, 0))
    in_specs = [tile(d)] + [tile(a.shape[-1]) for a in ins]
    args = [y] + list(ins)
    in_specs += [_resident(wo.shape), _resident((1, d)), _resident(w1.shape), _resident(w2.shape)]
    args += [wo, fg, w1, w2]
    if fn is not None:
        in_specs.append(_resident((1, d)))
        args.append(fn)
    return pl.pallas_call(
        functools.partial(_outproj_ffn_kernel, n_in=len(ins), final=fn is not None),
        grid=(b, t // tm),
        in_specs=in_specs,
        out_specs=tile(d),
        out_shape=jax.ShapeDtypeStruct((b, t, d), F32),
        compiler_params=_params(),
        name="outproj_ffn",
    )(*args)


def _hgrn_tables(length, width):
    nlev = int(np.log2(length))
    assert 2 ** nlev == length
    r = np.arange(length)
    sums = np.zeros((nlev + 1, length, length), np.float32)
    qmask = np.zeros((nlev, length, width), np.float32)
    pmask = np.zeros((nlev + 1, length, length), np.float32)
    for lv in range(nlev):
        hs = length >> (lv + 1)
        blk = r // (2 * hs)
        bnd = blk * 2 * hs + hs - 1
        is_q = (r % (2 * hs)) >= hs
        c = r[None, :]
        q_rows = (c > bnd[:, None]) & (c <= r[:, None])
        k_rows = (c > r[:, None]) & (c <= bnd[:, None])
        sums[lv] = np.where(is_q[:, None], q_rows, k_rows)
        qmask[lv] = is_q[:, None]
        pmask[lv] = (blk[:, None] == blk[None, :]) & is_q[:, None] & ~is_q[None, :]
    sums[nlev] = r[None, :] <= r[:, None]
    pmask[nlev] = np.eye(length)
    sums = sums.reshape((nlev + 1) * length, length)
    return (nlev, np.concatenate([sums, sums], axis=1), qmask,
            np.concatenate([pmask, pmask], axis=2))


def _pair_blocks(a):
    z = jnp.zeros((a.shape[0], C_DIM), a.dtype)
    return jnp.concatenate([jnp.concatenate([a[:, :C_DIM], z], axis=1),
                            jnp.concatenate([z, a[:, C_DIM:]], axis=1)], axis=0)


def _hgrn_kernel(x_ref, g_ref, w_ref, lbp_ref, s0_ref, gn_ref, sums_ref, qm_ref, pm_ref,
                 og_ref, sout_ref, z_scr, st_scr, *, tm, length, nlev, width):
    j = pl.program_id(1)
    h = _rms(x_ref[0], g_ref[...]).astype(BF16)
    z_scr[...] = jnp.dot(h, w_ref[...], preferred_element_type=F32)

    @pl.when(j == 0)
    def _():
        for hd in range(C_HEADS):
            st_scr[hd] = s0_ref[0, hd].T

    lbp = lbp_ref[...]
    mx = jnp.max(lbp, axis=0, keepdims=True)
    e = jnp.exp(lbp - mx)
    den = e[0:1] + e[1:2]
    s_first = e[0:1] / den
    lb = (s_first + e[1:2] / den) - s_first
    pair_w = 2 * C_DIM

    for r0 in range(0, tm, length):
        rows = slice(r0, r0 + length)
        q = z_scr[rows, 0:width]
        f = lb + (1.0 - lb) * jax.nn.sigmoid(z_scr[rows, width:2 * width])
        v = z_scr[rows, 2 * width:3 * width]
        logf = jnp.log(f)
        kk = 1.0 - f
        hi = logf.astype(BF16)
        mid = (logf - hi.astype(F32)).astype(BF16)
        dall = jnp.dot(sums_ref[...], jnp.concatenate([hi, mid], axis=0),
                       preferred_element_type=F32)
        gcum = dall[nlev * length:(nlev + 1) * length]
        qb = q.astype(BF16)
        kb = kk.astype(BF16)
        vb = v.astype(BF16)
        xs = []
        for lv in range(nlev):
            hs = length >> (lv + 1)
            ef = jnp.exp(dall[lv * length:(lv + 1) * length])
            if hs % 8 == 0:
                qk = jnp.concatenate(
                    [(q if i % 2 else kk)[i * hs:(i + 1) * hs] for i in range(length // hs)], axis=0)
            else:
                qk = jnp.where(qm_ref[lv] != 0.0, q, kk)
            xs.append((ef * qk).astype(BF16))
        qt = (q * jnp.exp(gcum)).astype(BF16)
        outs = []
        for pr in range(C_HEADS // 2):
            ps = slice(pr * pair_w, (pr + 1) * pair_w)
            amat = lax.dot_general(qb[:, ps], _pair_blocks(kb[:, ps]), _NT,
                                   preferred_element_type=F32) * pm_ref[nlev]
            for lv in range(nlev):
                xp = xs[lv][:, ps]
                amat = amat + lax.dot_general(xp, _pair_blocks(xp), _NT,
                                              preferred_element_type=F32) * pm_ref[lv]
            z128 = jnp.zeros((C_DIM, C_DIM), F32)
            st2 = jnp.concatenate(
                [jnp.concatenate([st_scr[2 * pr], z128], axis=1),
                 jnp.concatenate([z128, st_scr[2 * pr + 1]], axis=1)], axis=0).astype(BF16)
            outs.append(
                jnp.dot(amat.astype(BF16), _pair_blocks(vb[:, ps]), preferred_element_type=F32)
                + lax.dot_general(qt[:, ps], st2, _NT, preferred_element_type=F32))
        glast = gcum[length - 1:length, :]
        kdb = (kk * jnp.exp(glast - gcum)).astype(BF16)
        eg = jnp.exp(glast)
        for hd in range(C_HEADS):
            hsl = slice(hd * C_DIM, (hd + 1) * C_DIM)
            st_scr[hd] = st_scr[hd] * eg[:, hsl] + lax.dot_general(
                vb[:, hsl], kdb[:, hsl], _TN, preferred_element_type=F32)
        o = jnp.concatenate(outs, axis=1)
        inv = jnp.concatenate(
            [jnp.broadcast_to(
                lax.rsqrt(jnp.mean(jnp.square(o[:, hd * C_DIM:(hd + 1) * C_DIM]), axis=-1,
                                   keepdims=True) + EPS), (length, C_DIM))
             for hd in range(C_HEADS)], axis=1)
        gx = z_scr[rows, 3 * width:4 * width]
        og_ref[0, rows, :] = (o * inv * gn_ref[...] * (gx * jax.nn.sigmoid(gx))).astype(BF16)

    @pl.when(j == pl.num_programs(1) - 1)
    def _():
        for hd in range(C_HEADS):
            sout_ref[0, hd] = st_scr[hd].T


def _hgrn(x, g, w, lbp, s0, gn):
    b, t, d = x.shape
    width = w.shape[1] // 4
    assert width == C_HEADS * C_DIM
    tm = HGRN_ROWS if t % HGRN_ROWS == 0 else _pick_tm(t)
    length = min(CHUNK, t)
    nlev, sums, qmask, pmask = _hgrn_tables(length, width)
    tile = lambda c: pl.BlockSpec((1, tm, c), lambda i, j: (i, j, 0))
    state = pl.BlockSpec((1, C_HEADS, C_DIM, C_DIM), lambda i, j: (i, 0, 0, 0))
    return pl.pallas_call(
        functools.partial(_hgrn_kernel, tm=tm, length=length, nlev=nlev, width=width),
        grid=(b, t // tm),
        in_specs=[
            tile(d),
            _resident((1, d)),
            _resident(w.shape),
            _resident(lbp.shape),
            state,
            _resident((1, width)),
            _resident(sums.shape),
            _resident(qmask.shape),
            _resident(pmask.shape),
        ],
        out_specs=[tile(width), state],
        out_shape=[
            jax.ShapeDtypeStruct((b, t, width), BF16),
            jax.ShapeDtypeStruct((b, C_HEADS, C_DIM, C_DIM), F32),
        ],
        scratch_shapes=[
            pltpu.VMEM((tm, 4 * width), F32),
            pltpu.VMEM((C_HEADS, C_DIM, C_DIM), F32),
        ],
        compiler_params=_params(),
        name="inproj_hgrn",
    )(x, g, w, lbp, s0, jnp.tile(gn, (1, C_HEADS)), jnp.asarray(sums, BF16), jnp.asarray(qmask),
      jnp.asarray(pmask))


def _rel_bias_table(rel_bias):
    period = Q_ROWS + WIN
    offs = np.arange(period) - (Q_ROWS - 1)
    idx = np.clip(BAND_PAST - offs, -MAX_REL, MAX_REL) + MAX_REL
    u = rel_bias[:, idx].astype(F32)
    u = u - rel_bias[:, 2 * MAX_REL:].astype(F32)
    skew = jnp.tile(u, (1, Q_ROWS))[:, :Q_ROWS * (period - 1)].reshape(-1, Q_ROWS, period - 1)
    toep = skew[:, :, Q_ROWS - 1:Q_ROWS - 1 + WIN]
    first = (np.arange(Q_ROWS) // CHUNK * CHUNK)[:, None]
    col = np.arange(WIN)[None, :]
    return jnp.where((col >= first) & (col < first + BAND), toep, NEG)


def _stream(x, cache_k, cache_v, conv_hist, s0, p):
    b, t, d = x.shape
    bw = p["conv_w"].shape[-1]
    row = lambda a: a.reshape(1, -1)
    if cache_k is None:
        hist = jnp.zeros((b, HIST_PAD, bw), F32)
    else:
        hist = jnp.pad(conv_hist, ((0, 0), (HIST_OFF, 0), (0, 0)))
    q, k, v, kf, vf, co, ut = _inproj_ab(
        x, row(p["attn_norm"][0]), p["w_in_ab"], hist, p["conv_w"], row(p["conv_b"]),
        row(p["conv_ln_g"]), row(p["conv_ln_b"]))
    if cache_k is None:
        att = _attention(q, k, v, p["bias"], 0)
    else:
        w = cache_k.shape[1]
        assert w == BAND_PAST and t <= CHUNK
        pad = ((0, 0), (0, Q_ROWS - t), (0, 0))
        kp = jnp.concatenate([cache_k.reshape(b, w, A_WIDTH).astype(BF16), jnp.pad(k, pad)], axis=1)
        vp = jnp.concatenate([cache_v.reshape(b, w, A_WIDTH).astype(BF16), jnp.pad(v, pad)], axis=1)
        bias = jnp.where(np.arange(WIN)[None, None, :] < w + t, p["bias"], NEG)
        att = _attention(jnp.pad(q, pad), kp, vp, bias, w)[:, :t]
    y = _outproj_ffn(x, [att, co], p["w_out_ab"], row(p["ffn_norm"][0]), p["ffn_w_in"][0],
                     p["ffn_w_out"][0])
    keep = kf.shape[1]
    new_k = kf.reshape(1, b, keep, A_HEADS, A_HEAD_DIM)
    new_v = vf.reshape(1, b, keep, A_HEADS, A_HEAD_DIM)
    new_conv = ut[None, :, HIST_OFF:]
    if s0 is None:
        s0 = jnp.zeros((b, C_HEADS, C_DIM, C_DIM), F32)
    og, s_new = _hgrn(y, row(p["attn_norm"][1]), p["w_in_c"], p["lower_bounds"], s0,
                      row(p["c_norm_g"]))
    y = _outproj_ffn(y, [og], p["w_out_c"], row(p["ffn_norm"][1]), p["ffn_w_in"][1],
                     p["ffn_w_out"][1], fn=row(p["final_norm"]))
    return y, new_k, new_v, new_conv, s_new[None]


def kernel(x_prompt, x_sample, cache_a_k, cache_a_v, state_conv, state_hgrn, attn_norm, ffn_norm, final_norm, w_in_ab, rel_bias, conv_w, conv_b, conv_ln_g, conv_ln_b, w_out_ab, w_in_c, lower_bounds, c_norm_g, w_out_c, ffn_w_in, ffn_w_out):
    assert attn_norm.shape[0] == 2 and w_in_ab.shape[0] == 1 and w_in_c.shape[0] == 1
    p = dict(
        attn_norm=attn_norm, ffn_norm=ffn_norm, final_norm=final_norm,
        w_in_ab=w_in_ab[0].astype(BF16),
        bias=_rel_bias_table(rel_bias[0]),
        conv_w=jnp.broadcast_to(conv_w[0][:, None, :], (CONV_WIDTH, SUBLANES, conv_w.shape[-1])),
        conv_b=conv_b[0], conv_ln_g=conv_ln_g[0], conv_ln_b=conv_ln_b[0],
        w_out_ab=w_out_ab[0].astype(BF16),
        w_in_c=w_in_c[0].astype(BF16),
        lower_bounds=lower_bounds,
        c_norm_g=c_norm_g[0],
        w_out_c=w_out_c[0].astype(BF16),
        ffn_w_in=ffn_w_in.astype(BF16),
        ffn_w_out=ffn_w_out.astype(BF16),
    )
    yp, akp, avp, cvp, hgp = _stream(x_prompt, None, None, None, None, p)
    ys, aks, avs, cvs, hgs = _stream(x_sample, cache_a_k[0], cache_a_v[0], state_conv[0],
                                     state_hgrn[0], p)
    return (yp, ys, akp, avp, cvp, hgp, aks, avs, cvs, hgs)
```

```python
import functools

import numpy as np
import jax
import jax.numpy as jnp
from jax import lax
from jax.experimental import pallas as pl
from jax.experimental.pallas import tpu as pltpu

EPS = 1e-6
NEG = -1e30
CHUNK = 64
BAND_PAST = 512
BAND = BAND_PAST + CHUNK
Q_ROWS = 4 * CHUNK
WIN = BAND_PAST + Q_ROWS
A_HEADS = 8
A_HEAD_DIM = 64
A_WIDTH = A_HEADS * A_HEAD_DIM
MAX_REL = 128
MXU_COLS = 256
LANES = 128
SUBLANES = 8
BIAS_FREE = BAND_PAST - MAX_REL
CONV_WIDTH = 31
C_HEADS = 8
C_DIM = 128
HIST_ROWS = CONV_WIDTH - 1
HIST_PAD = 32
HIST_OFF = HIST_PAD - HIST_ROWS
CONV_ROWS = 32
FFN_ROWS = 512
HGRN_ROWS = 512
INPROJ_ROWS = 512
VMEM_LIMIT_BYTES = 56 * 1024 * 1024
BF16 = jnp.bfloat16
F32 = jnp.float32

_NT = (((1,), (1,)), ((), ()))
_TN = (((0,), (0,)), ((), ()))


def _rms(x, g):
    return x * lax.rsqrt(jnp.mean(x * x, axis=-1, keepdims=True) + EPS) * g


def _resident(shape):
    n = len(shape)
    return pl.BlockSpec(shape, lambda *_: (0,) * n, pipeline_mode=pl.Buffered(1))


def _params():
    return pltpu.CompilerParams(
        dimension_semantics=("arbitrary", "arbitrary"),
        vmem_limit_bytes=VMEM_LIMIT_BYTES,
    )


def _pick_tm(t):
    for tm in (256, 128, 64):
        if t % tm == 0:
            return tm
    return t


def _inproj_ab_kernel(x_ref, g_ref, w_ref, hist_ref, cw_ref, cb_ref, lng_ref, lnb_ref,
                      q_ref, k_ref, v_ref, kf_ref, vf_ref, co_ref, ut_ref, uext_ref,
                      cacc_ref, h_ref, *, tm, bw, rb):
    j = pl.program_id(1)
    h = _rms(x_ref[0], g_ref[...]).astype(BF16)
    zc = jnp.dot(h, w_ref[:, 3 * A_WIDTH:], preferred_element_type=F32)
    u = zc[:, :bw] * jax.nn.sigmoid(zc[:, bw:])

    @pl.when(j == 0)
    def _():
        uext_ref[0:HIST_PAD, :] = hist_ref[0]

    uext_ref[HIST_PAD:HIST_PAD + tm, :] = u
    h_ref[...] = h

    def project(c0, c1):
        z = jnp.dot(h_ref[...], w_ref[:, c0:c1], preferred_element_type=F32)
        if c0 < A_WIDTH:
            q_ref[0, :, c0:c1] = (z * (A_HEAD_DIM ** -0.5)).astype(BF16)
        elif c0 < 2 * A_WIDTH:
            k_ref[0, :, c0 - A_WIDTH:c1 - A_WIDTH] = z.astype(BF16)
            kf_ref[0, :, c0 - A_WIDTH:c1 - A_WIDTH] = z
        else:
            v_ref[0, :, c0 - 2 * A_WIDTH:c1 - 2 * A_WIDTH] = z.astype(BF16)
            vf_ref[0, :, c0 - 2 * A_WIDTH:c1 - 2 * A_WIDTH] = z

    slices = list(range(0, 3 * A_WIDTH, MXU_COLS))

    def conv_block(i):
        if i < len(slices):
            project(slices[i], slices[i] + MXU_COLS)
        r0 = i * rb
        for l0 in range(0, bw, LANES):
            lanes = slice(l0, l0 + LANES)
            x = uext_ref[r0:r0 + rb + HIST_PAD, lanes]
            acc = jnp.broadcast_to(cb_ref[:, lanes], (rb, LANES))
            for s in range(SUBLANES):
                xs = x if s == 0 else x[s:s + rb + HIST_PAD - SUBLANES]
                for a in range(HIST_PAD // SUBLANES + 1):
                    t = a * SUBLANES + s - HIST_OFF
                    if 0 <= t < CONV_WIDTH:
                        wt = jnp.concatenate([cw_ref[t, :, lanes]] * (rb // SUBLANES), axis=0)
                        acc = acc + wt * xs[a * SUBLANES:a * SUBLANES + rb]
            cacc_ref[r0:r0 + rb, lanes] = acc

    for i in range(tm // rb):
        pl.when(j >= 0)(functools.partial(conv_block, i))
    for c0 in slices[tm // rb:]:
        project(c0, c0 + MXU_COLS)
    y = cacc_ref[...]
    mu = jnp.mean(y, axis=-1, keepdims=True)
    d = y - mu
    var = jnp.mean(d * d, axis=-1, keepdims=True)
    y = d * lax.rsqrt(var + EPS) * lng_ref[...] + lnb_ref[...]
    co_ref[0] = (y * jax.nn.sigmoid(y)).astype(BF16)
    tail = uext_ref[tm:tm + HIST_PAD, :]
    ut_ref[0] = tail
    uext_ref[0:HIST_PAD, :] = tail


def _inproj_ab(x, g, w, hist, cw, cb, lng, lnb):
    b, t, d = x.shape
    n = w.shape[1]
    bw = (n - 3 * A_WIDTH) // 2
    tm = INPROJ_ROWS if t % INPROJ_ROWS == 0 else _pick_tm(t)
    nt = t // tm
    keep = min(BAND_PAST, t)
    first_keep = nt - keep // tm
    rb = CONV_ROWS if tm % CONV_ROWS == 0 else tm
    tile = lambda c: pl.BlockSpec((1, tm, c), lambda i, j: (i, j, 0))
    keep_spec = pl.BlockSpec((1, tm, A_WIDTH), lambda i, j: (i, jnp.maximum(j - first_keep, 0), 0))
    return pl.pallas_call(
        functools.partial(_inproj_ab_kernel, tm=tm, bw=bw, rb=rb),
        grid=(b, nt),
        in_specs=[
            tile(d),
            _resident((1, d)),
            _resident((d, n)),
            pl.BlockSpec((1, HIST_PAD, bw), lambda i, j: (i, 0, 0)),
            _resident((CONV_WIDTH, SUBLANES, bw)),
            _resident((1, bw)),
            _resident((1, bw)),
            _resident((1, bw)),
        ],
        out_specs=[
            tile(A_WIDTH), tile(A_WIDTH), tile(A_WIDTH),
            keep_spec, keep_spec,
            tile(bw),
            pl.BlockSpec((1, HIST_PAD, bw), lambda i, j: (i, 0, 0)),
        ],
        out_shape=[
            jax.ShapeDtypeStruct((b, t, A_WIDTH), BF16),
            jax.ShapeDtypeStruct((b, t, A_WIDTH), BF16),
            jax.ShapeDtypeStruct((b, t, A_WIDTH), BF16),
            jax.ShapeDtypeStruct((b, keep, A_WIDTH), F32),
            jax.ShapeDtypeStruct((b, keep, A_WIDTH), F32),
            jax.ShapeDtypeStruct((b, t, bw), BF16),
            jax.ShapeDtypeStruct((b, HIST_PAD, bw), F32),
        ],
        scratch_shapes=[
            pltpu.VMEM((tm + HIST_PAD, bw), F32),
            pltpu.VMEM((tm, bw), F32),
            pltpu.VMEM((tm, d), BF16),
        ],
        compiler_params=_params(),
        name="inproj_ab",
    )(x, g, w, hist, cw, cb, lng, lnb)


def _attn_group(q_ref, k_ref, v_ref, b_ref, o_ref, kstart, wlen):
    q = q_ref[0]
    kw = k_ref[0, pl.ds(kstart, wlen), :]
    vw = v_ref[0, pl.ds(kstart, wlen), :]
    off = WIN - wlen
    low = lax.broadcasted_iota(jnp.int32, (1, LANES), 1) < A_HEAD_DIM
    head_lanes = (low.astype(BF16), (~low).astype(BF16))
    ones = jnp.ones((wlen, LANES), BF16)
    outs = []
    for h in range(A_HEADS):
        sl = slice(h // 2 * LANES, (h // 2 + 1) * LANES)
        s = lax.dot_general(q[:, sl] * head_lanes[h % 2], kw[:, sl], _NT,
                            preferred_element_type=F32)
        probs = []
        for g in range(Q_ROWS // CHUNK):
            rows = slice(g * CHUNK, (g + 1) * CHUNK)
            lo = g * CHUNK
            tiles = []
            for c0 in range(off, WIN, LANES):
                c1 = c0 + LANES
                if c1 <= lo or c0 >= lo + BAND:
                    tiles.append(None)
                    continue
                st = s[rows, c0 - off:c1 - off]
                if c0 < lo or c1 > lo + BIAS_FREE:
                    st = st + b_ref[h, rows, c0:c1]
                tiles.append(st)
            live = [t for t in tiles if t is not None]
            m = jnp.max(functools.reduce(jnp.maximum, live), axis=-1, keepdims=True)
            probs.append(jnp.concatenate(
                [jnp.zeros((CHUNK, LANES), BF16) if t is None else jnp.exp(t - m).astype(BF16)
                 for t in tiles], axis=1))
        p = jnp.concatenate(probs, axis=0)
        pv = jnp.dot(p, jnp.concatenate([vw[:, sl], ones], axis=1), preferred_element_type=F32)
        o = pv[:, :LANES] / pv[:, LANES:]
        if h % 2:
            outs.append(jnp.where(low, outs.pop(), o))
        else:
            outs.append(o)
    o_ref[0] = jnp.concatenate(outs, axis=-1).astype(BF16)


def _attn_kernel(q_ref, k_ref, v_ref, b_ref, o_ref, *, lead):
    g = pl.program_id(1)
    n_short = (BAND_PAST - lead) // Q_ROWS
    for i in range(n_short):
        @pl.when(g == i)
        def _(i=i):
            _attn_group(q_ref, k_ref, v_ref, b_ref, o_ref, 0, lead + (i + 1) * Q_ROWS)

    @pl.when(g >= n_short)
    def _():
        kstart = pl.multiple_of((g - n_short) * Q_ROWS, Q_ROWS)
        _attn_group(q_ref, k_ref, v_ref, b_ref, o_ref, kstart, WIN)


def _attention(q, k, v, bias, lead):
    b, tq, _ = q.shape
    tk = k.shape[1]
    assert tq % Q_ROWS == 0 and tk == lead + tq and (BAND_PAST - lead) % Q_ROWS == 0
    return pl.pallas_call(
        functools.partial(_attn_kernel, lead=lead),
        grid=(b, tq // Q_ROWS),
        in_specs=[
            pl.BlockSpec((1, Q_ROWS, A_WIDTH), lambda i, j: (i, j, 0)),
            pl.BlockSpec((1, tk, A_WIDTH), lambda i, j: (i, 0, 0)),
            pl.BlockSpec((1, tk, A_WIDTH), lambda i, j: (i, 0, 0)),
            _resident((A_HEADS, Q_ROWS, WIN)),
        ],
        out_specs=pl.BlockSpec((1, Q_ROWS, A_WIDTH), lambda i, j: (i, j, 0)),
        out_shape=jax.ShapeDtypeStruct((b, tq, A_WIDTH), BF16),
        compiler_params=_params(),
        name="band_attention",
    )(q, k, v, bias)


def _outproj_ffn_kernel(*refs, n_in, final):
    y_ref = refs[0]
    ins = refs[1:1 + n_in]
    wo_ref, fg_ref, w1_ref, w2_ref = refs[1 + n_in:5 + n_in]
    fn_ref = refs[5 + n_in] if final else None
    o_ref = refs[-1]
    y = y_ref[0]
    r0 = 0
    for in_ref in ins:
        r1 = r0 + in_ref.shape[-1]
        y = y + jnp.dot(in_ref[0], wo_ref[r0:r1, :], preferred_element_type=F32)
        r0 = r1
    hn = _rms(y, fg_ref[...]).astype(BF16)
    hh = jnp.dot(hn, w1_ref[...], preferred_element_type=F32)
    dff = w2_ref.shape[0]
    a = hh[:, :dff]
    act = (a * jax.nn.sigmoid(a) * hh[:, dff:]).astype(BF16)
    y = y + jnp.dot(act, w2_ref[...], preferred_element_type=F32)
    if final:
        y = _rms(y, fn_ref[...])
    o_ref[0] = y


def _outproj_ffn(y, ins, wo, fg, w1, w2, fn=None):
    b, t, d = y.shape
    assert sum(a.shape[-1] for a in ins) == wo.shape[0]
    tm = FFN_ROWS if t % FFN_ROWS == 0 else _pick_tm(t)
    tile = lambda c: pl.BlockSpec((1, tm, c), lambda i, j: (i, j, 0))
    in_specs = [tile(d)] + [tile(a.shape[-1]) for a in ins]
    args = [y] + list(ins)
    in_specs += [_resident(wo.shape), _resident((1, d)), _resident(w1.shape), _resident(w2.shape)]
    args += [wo, fg, w1, w2]
    if fn is not None:
        in_specs.append(_resident((1, d)))
        args.append(fn)
    return pl.pallas_call(
        functools.partial(_outproj_ffn_kernel, n_in=len(ins), final=fn is not None),
        grid=(b, t // tm),
        in_specs=in_specs,
        out_specs=tile(d),
        out_shape=jax.ShapeDtypeStruct((b, t, d), F32),
        compiler_params=_params(),
        name="outproj_ffn",
    )(*args)


def _hgrn_tables(length, width):
    nlev = int(np.log2(length))
    assert 2 ** nlev == length
    r = np.arange(length)
    sums = np.zeros((nlev + 1, length, length), np.float32)
    qmask = np.zeros((nlev, length, width), np.float32)
    pmask = np.zeros((nlev + 1, length, length), np.float32)
    for lv in range(nlev):
        hs = length >> (lv + 1)
        blk = r // (2 * hs)
        bnd = blk * 2 * hs + hs - 1
        is_q = (r % (2 * hs)) >= hs
        c = r[None, :]
        q_rows = (c > bnd[:, None]) & (c <= r[:, None])
        k_rows = (c > r[:, None]) & (c <= bnd[:, None])
        sums[lv] = np.where(is_q[:, None], q_rows, k_rows)
        qmask[lv] = is_q[:, None]
        pmask[lv] = (blk[:, None] == blk[None, :]) & is_q[:, None] & ~is_q[None, :]
    sums[nlev] = r[None, :] <= r[:, None]
    pmask[nlev] = np.eye(length)
    table_levels = [lv for lv in range(nlev) if 1 < (length >> (lv + 1)) < SUBLANES]
    sums = sums[table_levels + [nlev]].reshape(-1, length)
    return (nlev, table_levels, np.concatenate([sums, sums], axis=1), qmask,
            np.concatenate([pmask, pmask], axis=2))


def _pair_blocks(a):
    z = jnp.zeros((a.shape[0], C_DIM), a.dtype)
    return jnp.concatenate([jnp.concatenate([a[:, :C_DIM], z], axis=1),
                            jnp.concatenate([z, a[:, C_DIM:]], axis=1)], axis=0)


def _hgrn_kernel(x_ref, g_ref, w_ref, lbp_ref, s0_ref, gn_ref, sums_ref, qm_ref, pm_ref,
                 og_ref, sout_ref, z_scr, st_scr, *, tm, length, nlev, table_levels, width):
    j = pl.program_id(1)
    h = _rms(x_ref[0], g_ref[...]).astype(BF16)
    z_scr[...] = jnp.dot(h, w_ref[...], preferred_element_type=F32)

    @pl.when(j == 0)
    def _():
        for hd in range(C_HEADS):
            st_scr[hd] = s0_ref[0, hd].T

    lbp = lbp_ref[...]
    mx = jnp.max(lbp, axis=0, keepdims=True)
    e = jnp.exp(lbp - mx)
    den = e[0:1] + e[1:2]
    s_first = e[0:1] / den
    lb = (s_first + e[1:2] / den) - s_first
    pair_w = 2 * C_DIM

    for r0 in range(0, tm, length):
        rows = slice(r0, r0 + length)
        q = z_scr[rows, 0:width]
        f = lb + (1.0 - lb) * jax.nn.sigmoid(z_scr[rows, width:2 * width])
        v = z_scr[rows, 2 * width:3 * width]
        logf = jnp.log(f)
        kk = 1.0 - f
        hi = logf.astype(BF16)
        mid = (logf - hi.astype(F32)).astype(BF16)
        dall = jnp.dot(sums_ref[...], jnp.concatenate([hi, mid], axis=0),
                       preferred_element_type=F32)
        gcum = dall[len(table_levels) * length:]
        qb = q.astype(BF16)
        kb = kk.astype(BF16)
        vb = v.astype(BF16)
        xs = []
        for lv in range(nlev):
            hs = length >> (lv + 1)
            if hs % SUBLANES == 0:
                qk, dl = [], []
                for i in range(length // hs):
                    part = slice(i * hs, (i + 1) * hs)
                    bnd = (i // 2) * 2 * hs + hs - 1
                    gb = gcum[bnd:bnd + 1]
                    qk.append((q if i % 2 else kk)[part])
                    dl.append(gcum[part] - gb if i % 2 else gb - gcum[part])
                x = jnp.exp(jnp.concatenate(dl, axis=0)) * jnp.concatenate(qk, axis=0)
            elif hs == 1:
                x = jnp.where(qm_ref[lv] != 0.0, q * f, kk)
            else:
                slab = table_levels.index(lv)
                x = jnp.exp(dall[slab * length:(slab + 1) * length]) * jnp.where(
                    qm_ref[lv] != 0.0, q, kk)
            xs.append(x.astype(BF16))
        qt = (q * jnp.exp(gcum)).astype(BF16)
        outs = []
        for pr in range(C_HEADS // 2):
            ps = slice(pr * pair_w, (pr + 1) * pair_w)
            amat = lax.dot_general(qb[:, ps], _pair_blocks(kb[:, ps]), _NT,
                                   preferred_element_type=F32) * pm_ref[nlev]
            for lv in range(nlev):
                xp = xs[lv][:, ps]
                amat = amat + lax.dot_general(xp, _pair_blocks(xp), _NT,
                                              preferred_element_type=F32) * pm_ref[lv]
            z128 = jnp.zeros((C_DIM, C_DIM), F32)
            st2 = jnp.concatenate(
                [jnp.concatenate([st_scr[2 * pr], z128], axis=1),
                 jnp.concatenate([z128, st_scr[2 * pr + 1]], axis=1)], axis=0).astype(BF16)
            outs.append(
                jnp.dot(amat.astype(BF16), _pair_blocks(vb[:, ps]), preferred_element_type=F32)
                + lax.dot_general(qt[:, ps], st2, _NT, preferred_element_type=F32))
        glast = gcum[length - 1:length, :]
        kdb = (kk * jnp.exp(glast - gcum)).astype(BF16)
        eg = jnp.exp(glast)
        for hd in range(C_HEADS):
            hsl = slice(hd * C_DIM, (hd + 1) * C_DIM)
            st_scr[hd] = st_scr[hd] * eg[:, hsl] + lax.dot_general(
                vb[:, hsl], kdb[:, hsl], _TN, preferred_element_type=F32)
        o = jnp.concatenate(outs, axis=1)
        inv = jnp.concatenate(
            [jnp.broadcast_to(
                lax.rsqrt(jnp.mean(jnp.square(o[:, hd * C_DIM:(hd + 1) * C_DIM]), axis=-1,
                                   keepdims=True) + EPS), (length, C_DIM))
             for hd in range(C_HEADS)], axis=1)
        gx = z_scr[rows, 3 * width:4 * width]
        og_ref[0, rows, :] = (o * inv * gn_ref[...] * (gx * jax.nn.sigmoid(gx))).astype(BF16)

    @pl.when(j == pl.num_programs(1) - 1)
    def _():
        for hd in range(C_HEADS):
            sout_ref[0, hd] = st_scr[hd].T


def _hgrn(x, g, w, lbp, s0, gn):
    b, t, d = x.shape
    width = w.shape[1] // 4
    assert width == C_HEADS * C_DIM
    tm = HGRN_ROWS if t % HGRN_ROWS == 0 else _pick_tm(t)
    length = min(CHUNK, t)
    nlev, table_levels, sums, qmask, pmask = _hgrn_tables(length, width)
    tile = lambda c: pl.BlockSpec((1, tm, c), lambda i, j: (i, j, 0))
    state = pl.BlockSpec((1, C_HEADS, C_DIM, C_DIM), lambda i, j: (i, 0, 0, 0))
    return pl.pallas_call(
        functools.partial(_hgrn_kernel, tm=tm, length=length, nlev=nlev,
                          table_levels=table_levels, width=width),
        grid=(b, t // tm),
        in_specs=[
            tile(d),
            _resident((1, d)),
            _resident(w.shape),
            _resident(lbp.shape),
            state,
            _resident((1, width)),
            _resident(sums.shape),
            _resident(qmask.shape),
            _resident(pmask.shape),
        ],
        out_specs=[tile(width), state],
        out_shape=[
            jax.ShapeDtypeStruct((b, t, width), BF16),
            jax.ShapeDtypeStruct((b, C_HEADS, C_DIM, C_DIM), F32),
        ],
        scratch_shapes=[
            pltpu.VMEM((tm, 4 * width), F32),
            pltpu.VMEM((C_HEADS, C_DIM, C_DIM), F32),
        ],
        compiler_params=_params(),
        name="inproj_hgrn",
    )(x, g, w, lbp, s0, jnp.tile(gn, (1, C_HEADS)), jnp.asarray(sums, BF16), jnp.asarray(qmask),
      jnp.asarray(pmask))


def _rel_bias_table(rel_bias):
    period = Q_ROWS + WIN
    offs = np.arange(period) - (Q_ROWS - 1)
    idx = np.clip(BAND_PAST - offs, -MAX_REL, MAX_REL) + MAX_REL
    u = rel_bias[:, idx].astype(F32)
    u = u - rel_bias[:, 2 * MAX_REL:].astype(F32)
    skew = jnp.tile(u, (1, Q_ROWS))[:, :Q_ROWS * (period - 1)].reshape(-1, Q_ROWS, period - 1)
    toep = skew[:, :, Q_ROWS - 1:Q_ROWS - 1 + WIN]
    first = (np.arange(Q_ROWS) // CHUNK * CHUNK)[:, None]
    col = np.arange(WIN)[None, :]
    return jnp.where((col >= first) & (col < first + BAND), toep, NEG)


def _stream(x, cache_k, cache_v, conv_hist, s0, p):
    b, t, d = x.shape
    bw = p["conv_w"].shape[-1]
    row = lambda a: a.reshape(1, -1)
    if cache_k is None:
        hist = jnp.zeros((b, HIST_PAD, bw), F32)
    else:
        hist = jnp.pad(conv_hist, ((0, 0), (HIST_OFF, 0), (0, 0)))
    q, k, v, kf, vf, co, ut = _inproj_ab(
        x, row(p["attn_norm"][0]), p["w_in_ab"], hist, p["conv_w"], row(p["conv_b"]),
        row(p["conv_ln_g"]), row(p["conv_ln_b"]))
    if cache_k is None:
        att = _attention(q, k, v, p["bias"], 0)
    else:
        w = cache_k.shape[1]
        assert w == BAND_PAST and t <= CHUNK
        pad = ((0, 0), (0, Q_ROWS - t), (0, 0))
        kp = jnp.concatenate([cache_k.reshape(b, w, A_WIDTH).astype(BF16), jnp.pad(k, pad)], axis=1)
        vp = jnp.concatenate([cache_v.reshape(b, w, A_WIDTH).astype(BF16), jnp.pad(v, pad)], axis=1)
        bias = jnp.where(np.arange(WIN)[None, None, :] < w + t, p["bias"], NEG)
        att = _attention(jnp.pad(q, pad), kp, vp, bias, w)[:, :t]
    y = _outproj_ffn(x, [att, co], p["w_out_ab"], row(p["ffn_norm"][0]), p["ffn_w_in"][0],
                     p["ffn_w_out"][0])
    keep = kf.shape[1]
    new_k = kf.reshape(1, b, keep, A_HEADS, A_HEAD_DIM)
    new_v = vf.reshape(1, b, keep, A_HEADS, A_HEAD_DIM)
    new_conv = ut[None, :, HIST_OFF:]
    if s0 is None:
        s0 = jnp.zeros((b, C_HEADS, C_DIM, C_DIM), F32)
    og, s_new = _hgrn(y, row(p["attn_norm"][1]), p["w_in_c"], p["lower_bounds"], s0,
                      row(p["c_norm_g"]))
    y = _outproj_ffn(y, [og], p["w_out_c"], row(p["ffn_norm"][1]), p["ffn_w_in"][1],
                     p["ffn_w_out"][1], fn=row(p["final_norm"]))
    return y, new_k, new_v, new_conv, s_new[None]


def kernel(x_prompt, x_sample, cache_a_k, cache_a_v, state_conv, state_hgrn, attn_norm, ffn_norm, final_norm, w_in_ab, rel_bias, conv_w, conv_b, conv_ln_g, conv_ln_b, w_out_ab, w_in_c, lower_bounds, c_norm_g, w_out_c, ffn_w_in, ffn_w_out):
    assert attn_norm.shape[0] == 2 and w_in_ab.shape[0] == 1 and w_in_c.shape[0] == 1
    p = dict(
        attn_norm=attn_norm, ffn_norm=ffn_norm, final_norm=final_norm,
        w_in_ab=w_in_ab[0].astype(BF16),
        bias=_rel_bias_table(rel_bias[0]),
        conv_w=jnp.broadcast_to(conv_w[0][:, None, :], (CONV_WIDTH, SUBLANES, conv_w.shape[-1])),
        conv_b=conv_b[0], conv_ln_g=conv_ln_g[0], conv_ln_b=conv_ln_b[0],
        w_out_ab=w_out_ab[0].astype(BF16),
        w_in_c=w_in_c[0].astype(BF16),
        lower_bounds=lower_bounds,
        c_norm_g=c_norm_g[0],
        w_out_c=w_out_c[0].astype(BF16),
        ffn_w_in=ffn_w_in.astype(BF16),
        ffn_w_out=ffn_w_out.astype(BF16),
    )
    yp, akp, avp, cvp, hgp = _stream(x_prompt, None, None, None, None, p)
    ys, aks, avs, cvs, hgs = _stream(x_sample, cache_a_k[0], cache_a_v[0], state_conv[0],
                                     state_hgrn[0], p)
    return (yp, ys, akp, avp, cvp, hgp, aks, avs, cvs, hgs)
```

```python
import functools

import numpy as np
import jax
import jax.numpy as jnp
from jax import lax
from jax.experimental import pallas as pl
from jax.experimental.pallas import tpu as pltpu

EPS = 1e-6
NEG = -1e30
CHUNK = 64
BAND_PAST = 512
BAND = BAND_PAST + CHUNK
Q_ROWS = 4 * CHUNK
WIN = BAND_PAST + Q_ROWS
A_HEADS = 8
A_HEAD_DIM = 64
A_WIDTH = A_HEADS * A_HEAD_DIM
MAX_REL = 128
MXU_COLS = 256
LANES = 128
SUBLANES = 8
BIAS_FREE = BAND_PAST - MAX_REL
CONV_WIDTH = 31
C_HEADS = 8
C_DIM = 128
HIST_ROWS = CONV_WIDTH - 1
HIST_PAD = 32
HIST_OFF = HIST_PAD - HIST_ROWS
CONV_ROWS = 32
FFN_ROWS = 512
HGRN_ROWS = 1024
INPROJ_ROWS = 1024
VMEM_LIMIT_BYTES = 56 * 1024 * 1024
BF16 = jnp.bfloat16
F32 = jnp.float32

_NT = (((1,), (1,)), ((), ()))
_TN = (((0,), (0,)), ((), ()))


def _rms(x, g):
    return x * lax.rsqrt(jnp.mean(x * x, axis=-1, keepdims=True) + EPS) * g


def _resident(shape):
    n = len(shape)
    return pl.BlockSpec(shape, lambda *_: (0,) * n, pipeline_mode=pl.Buffered(1))


def _params():
    return pltpu.CompilerParams(
        dimension_semantics=("arbitrary", "arbitrary"),
        vmem_limit_bytes=VMEM_LIMIT_BYTES,
    )


def _pick_tm(t):
    for tm in (256, 128, 64):
        if t % tm == 0:
            return tm
    return t


def _inproj_ab_kernel(x_ref, g_ref, w_ref, hist_ref, cw_ref, cb_ref, lng_ref, lnb_ref,
                      q_ref, k_ref, v_ref, kf_ref, vf_ref, co_ref, ut_ref, uext_ref,
                      cacc_ref, h_ref, *, tm, bw, rb):
    j = pl.program_id(1)
    h = _rms(x_ref[0], g_ref[...]).astype(BF16)
    zc = jnp.dot(h, w_ref[:, 3 * A_WIDTH:], preferred_element_type=F32)
    u = zc[:, :bw] * jax.nn.sigmoid(zc[:, bw:])

    @pl.when(j == 0)
    def _():
        uext_ref[0:HIST_PAD, :] = hist_ref[0]

    uext_ref[HIST_PAD:HIST_PAD + tm, :] = u
    h_ref[...] = h

    def project(c0, c1):
        z = jnp.dot(h_ref[...], w_ref[:, c0:c1], preferred_element_type=F32)
        if c0 < A_WIDTH:
            q_ref[0, :, c0:c1] = (z * (A_HEAD_DIM ** -0.5)).astype(BF16)
        elif c0 < 2 * A_WIDTH:
            k_ref[0, :, c0 - A_WIDTH:c1 - A_WIDTH] = z.astype(BF16)
            kf_ref[0, :, c0 - A_WIDTH:c1 - A_WIDTH] = z[tm - kf_ref.shape[1]:]
        else:
            v_ref[0, :, c0 - 2 * A_WIDTH:c1 - 2 * A_WIDTH] = z.astype(BF16)
            vf_ref[0, :, c0 - 2 * A_WIDTH:c1 - 2 * A_WIDTH] = z[tm - vf_ref.shape[1]:]

    slices = list(range(0, 3 * A_WIDTH, MXU_COLS))

    def conv_block(i):
        if i < len(slices):
            project(slices[i], slices[i] + MXU_COLS)
        r0 = i * rb
        for l0 in range(0, bw, LANES):
            lanes = slice(l0, l0 + LANES)
            x = uext_ref[r0:r0 + rb + HIST_PAD, lanes]
            acc = jnp.broadcast_to(cb_ref[:, lanes], (rb, LANES))
            for s in range(SUBLANES):
                xs = x if s == 0 else x[s:s + rb + HIST_PAD - SUBLANES]
                for a in range(HIST_PAD // SUBLANES + 1):
                    t = a * SUBLANES + s - HIST_OFF
                    if 0 <= t < CONV_WIDTH:
                        wt = jnp.concatenate([cw_ref[t, :, lanes]] * (rb // SUBLANES), axis=0)
                        acc = acc + wt * xs[a * SUBLANES:a * SUBLANES + rb]
            cacc_ref[r0:r0 + rb, lanes] = acc

    for i in range(tm // rb):
        pl.when(j >= 0)(functools.partial(conv_block, i))
    for c0 in slices[tm // rb:]:
        project(c0, c0 + MXU_COLS)
    y = cacc_ref[...]
    mu = jnp.mean(y, axis=-1, keepdims=True)
    d = y - mu
    var = jnp.mean(d * d, axis=-1, keepdims=True)
    y = d * lax.rsqrt(var + EPS) * lng_ref[...] + lnb_ref[...]
    co_ref[0] = (y * jax.nn.sigmoid(y)).astype(BF16)
    tail = uext_ref[tm:tm + HIST_PAD, :]
    ut_ref[0] = tail
    uext_ref[0:HIST_PAD, :] = tail


def _inproj_ab(x, g, w, hist, cw, cb, lng, lnb):
    b, t, d = x.shape
    n = w.shape[1]
    bw = (n - 3 * A_WIDTH) // 2
    tm = INPROJ_ROWS if t % INPROJ_ROWS == 0 else _pick_tm(t)
    nt = t // tm
    keep = min(BAND_PAST, t)
    keep_rows = min(keep, tm)
    first_keep = nt - keep // keep_rows
    rb = CONV_ROWS if tm % CONV_ROWS == 0 else tm
    tile = lambda c: pl.BlockSpec((1, tm, c), lambda i, j: (i, j, 0))
    keep_spec = pl.BlockSpec((1, keep_rows, A_WIDTH),
                             lambda i, j: (i, jnp.maximum(j - first_keep, 0), 0))
    return pl.pallas_call(
        functools.partial(_inproj_ab_kernel, tm=tm, bw=bw, rb=rb),
        grid=(b, nt),
        in_specs=[
            tile(d),
            _resident((1, d)),
            _resident((d, n)),
            pl.BlockSpec((1, HIST_PAD, bw), lambda i, j: (i, 0, 0)),
            _resident((CONV_WIDTH, SUBLANES, bw)),
            _resident((1, bw)),
            _resident((1, bw)),
            _resident((1, bw)),
        ],
        out_specs=[
            tile(A_WIDTH), tile(A_WIDTH), tile(A_WIDTH),
            keep_spec, keep_spec,
            tile(bw),
            pl.BlockSpec((1, HIST_PAD, bw), lambda i, j: (i, 0, 0)),
        ],
        out_shape=[
            jax.ShapeDtypeStruct((b, t, A_WIDTH), BF16),
            jax.ShapeDtypeStruct((b, t, A_WIDTH), BF16),
            jax.ShapeDtypeStruct((b, t, A_WIDTH), BF16),
            jax.ShapeDtypeStruct((b, keep, A_WIDTH), F32),
            jax.ShapeDtypeStruct((b, keep, A_WIDTH), F32),
            jax.ShapeDtypeStruct((b, t, bw), BF16),
            jax.ShapeDtypeStruct((b, HIST_PAD, bw), F32),
        ],
        scratch_shapes=[
            pltpu.VMEM((tm + HIST_PAD, bw), F32),
            pltpu.VMEM((tm, bw), F32),
            pltpu.VMEM((tm, d), BF16),
        ],
        compiler_params=_params(),
        name="inproj_ab",
    )(x, g, w, hist, cw, cb, lng, lnb)


def _attn_group(q_ref, k_ref, v_ref, b_ref, o_ref, kstart, wlen):
    q = q_ref[0]
    kw = k_ref[0, pl.ds(kstart, wlen), :]
    vw = v_ref[0, pl.ds(kstart, wlen), :]
    off = WIN - wlen
    low = lax.broadcasted_iota(jnp.int32, (1, LANES), 1) < A_HEAD_DIM
    head_lanes = (low.astype(BF16), (~low).astype(BF16))
    ones = jnp.ones((wlen, LANES), BF16)
    outs = []
    for h in range(A_HEADS):
        sl = slice(h // 2 * LANES, (h // 2 + 1) * LANES)
        s = lax.dot_general(q[:, sl] * head_lanes[h % 2], kw[:, sl], _NT,
                            preferred_element_type=F32)
        probs = []
        for g in range(Q_ROWS // CHUNK):
            rows = slice(g * CHUNK, (g + 1) * CHUNK)
            lo = g * CHUNK
            tiles = []
            for c0 in range(off, WIN, LANES):
                c1 = c0 + LANES
                if c1 <= lo or c0 >= lo + BAND:
                    tiles.append(None)
                    continue
                st = s[rows, c0 - off:c1 - off]
                if c0 < lo or c1 > lo + BIAS_FREE:
                    st = st + b_ref[h, rows, c0:c1]
                tiles.append(st)
            live = [t for t in tiles if t is not None]
            m = jnp.max(functools.reduce(jnp.maximum, live), axis=-1, keepdims=True)
            probs.append(jnp.concatenate(
                [jnp.zeros((CHUNK, LANES), BF16) if t is None else jnp.exp(t - m).astype(BF16)
                 for t in tiles], axis=1))
        p = jnp.concatenate(probs, axis=0)
        pv = jnp.dot(p, jnp.concatenate([vw[:, sl], ones], axis=1), preferred_element_type=F32)
        o = pv[:, :LANES] / pv[:, LANES:]
        if h % 2:
            outs.append(jnp.where(low, outs.pop(), o))
        else:
            outs.append(o)
    o_ref[0] = jnp.concatenate(outs, axis=-1).astype(BF16)


def _attn_kernel(q_ref, k_ref, v_ref, b_ref, o_ref, *, lead):
    g = pl.program_id(1)
    n_short = (BAND_PAST - lead) // Q_ROWS
    for i in range(n_short):
        @pl.when(g == i)
        def _(i=i):
            _attn_group(q_ref, k_ref, v_ref, b_ref, o_ref, 0, lead + (i + 1) * Q_ROWS)

    @pl.when(g >= n_short)
    def _():
        kstart = pl.multiple_of((g - n_short) * Q_ROWS, Q_ROWS)
        _attn_group(q_ref, k_ref, v_ref, b_ref, o_ref, kstart, WIN)


def _attention(q, k, v, bias, lead):
    b, tq, _ = q.shape
    tk = k.shape[1]
    assert tq % Q_ROWS == 0 and tk == lead + tq and (BAND_PAST - lead) % Q_ROWS == 0
    return pl.pallas_call(
        functools.partial(_attn_kernel, lead=lead),
        grid=(b, tq // Q_ROWS),
        in_specs=[
            pl.BlockSpec((1, Q_ROWS, A_WIDTH), lambda i, j: (i, j, 0)),
            pl.BlockSpec((1, tk, A_WIDTH), lambda i, j: (i, 0, 0)),
            pl.BlockSpec((1, tk, A_WIDTH), lambda i, j: (i, 0, 0)),
            _resident((A_HEADS, Q_ROWS, WIN)),
        ],
        out_specs=pl.BlockSpec((1, Q_ROWS, A_WIDTH), lambda i, j: (i, j, 0)),
        out_shape=jax.ShapeDtypeStruct((b, tq, A_WIDTH), BF16),
        compiler_params=_params(),
        name="band_attention",
    )(q, k, v, bias)


def _outproj_ffn_kernel(*refs, n_in, final):
    y_ref = refs[0]
    ins = refs[1:1 + n_in]
    wo_ref, fg_ref, w1_ref, w2_ref = refs[1 + n_in:5 + n_in]
    fn_ref = refs[5 + n_in] if final else None
    o_ref = refs[-1]
    y = y_ref[0]
    r0 = 0
    for in_ref in ins:
        r1 = r0 + in_ref.shape[-1]
        y = y + jnp.dot(in_ref[0], wo_ref[r0:r1, :], preferred_element_type=F32)
        r0 = r1
    hn = _rms(y, fg_ref[...]).astype(BF16)
    hh = jnp.dot(hn, w1_ref[...], preferred_element_type=F32)
    dff = w2_ref.shape[0]
    a = hh[:, :dff]
    act = (a * jax.nn.sigmoid(a) * hh[:, dff:]).astype(BF16)
    y = y + jnp.dot(act, w2_ref[...], preferred_element_type=F32)
    if final:
        y = _rms(y, fn_ref[...])
    o_ref[0] = y


def _outproj_ffn(y, ins, wo, fg, w1, w2, fn=None):
    b, t, d = y.shape
    assert sum(a.shape[-1] for a in ins) == wo.shape[0]
    tm = FFN_ROWS if t % FFN_ROWS == 0 else _pick_tm(t)
    tile = lambda c: pl.BlockSpec((1, tm, c), lambda i, j: (i, j, 0))
    in_specs = [tile(d)] + [tile(a.shape[-1]) for a in ins]
    args = [y] + list(ins)
    in_specs += [_resident(wo.shape), _resident((1, d)), _resident(w1.shape), _resident(w2.shape)]
    args += [wo, fg, w1, w2]
    if fn is not None:
        in_specs.append(_resident((1, d)))
        args.append(fn)
    return pl.pallas_call(
        functools.partial(_outproj_ffn_kernel, n_in=len(ins), final=fn is not None),
        grid=(b, t // tm),
        in_specs=in_specs,
        out_specs=tile(d),
        out_shape=jax.ShapeDtypeStruct((b, t, d), F32),
        compiler_params=_params(),
        name="outproj_ffn",
    )(*args)


def _hgrn_tables(length, width):
    nlev = int(np.log2(length))
    assert 2 ** nlev == length
    r = np.arange(length)
    sums = np.zeros((nlev + 1, length, length), np.float32)
    qmask = np.zeros((nlev, length, width), np.float32)
    pmask = np.zeros((nlev + 1, length, length), np.float32)
    for lv in range(nlev):
        hs = length >> (lv + 1)
        blk = r // (2 * hs)
        bnd = blk * 2 * hs + hs - 1
        is_q = (r % (2 * hs)) >= hs
        c = r[None, :]
        q_rows = (c > bnd[:, None]) & (c <= r[:, None])
        k_rows = (c > r[:, None]) & (c <= bnd[:, None])
        sums[lv] = np.where(is_q[:, None], q_rows, k_rows)
        qmask[lv] = is_q[:, None]
        pmask[lv] = (blk[:, None] == blk[None, :]) & is_q[:, None] & ~is_q[None, :]
    sums[nlev] = r[None, :] <= r[:, None]
    pmask[nlev] = np.eye(length)
    table_levels = [lv for lv in range(nlev) if 1 < (length >> (lv + 1)) < SUBLANES]
    sums = sums[table_levels + [nlev]].reshape(-1, length)
    return (nlev, table_levels, np.concatenate([sums, sums], axis=1), qmask,
            np.concatenate([pmask, pmask], axis=2))


def _pair_blocks(a):
    z = jnp.zeros((a.shape[0], C_DIM), a.dtype)
    return jnp.concatenate([jnp.concatenate([a[:, :C_DIM], z], axis=1),
                            jnp.concatenate([z, a[:, C_DIM:]], axis=1)], axis=0)


def _hgrn_kernel(x_ref, g_ref, w_ref, lbp_ref, s0_ref, gn_ref, sums_ref, qm_ref, pm_ref,
                 og_ref, sout_ref, z_scr, st_scr, *, tm, length, nlev, table_levels, width):
    j = pl.program_id(1)
    h = _rms(x_ref[0], g_ref[...]).astype(BF16)
    z_scr[...] = jnp.dot(h, w_ref[...], preferred_element_type=F32)

    @pl.when(j == 0)
    def _():
        for hd in range(C_HEADS):
            st_scr[hd] = s0_ref[0, hd].T

    lbp = lbp_ref[...]
    mx = jnp.max(lbp, axis=0, keepdims=True)
    e = jnp.exp(lbp - mx)
    den = e[0:1] + e[1:2]
    s_first = e[0:1] / den
    lb = (s_first + e[1:2] / den) - s_first
    pair_w = 2 * C_DIM

    for r0 in range(0, tm, length):
        rows = slice(r0, r0 + length)
        q = z_scr[rows, 0:width]
        f = lb + (1.0 - lb) * jax.nn.sigmoid(z_scr[rows, width:2 * width])
        v = z_scr[rows, 2 * width:3 * width]
        logf = jnp.log(f)
        kk = 1.0 - f
        hi = logf.astype(BF16)
        mid = (logf - hi.astype(F32)).astype(BF16)
        dall = jnp.dot(sums_ref[...], jnp.concatenate([hi, mid], axis=0),
                       preferred_element_type=F32)
        gcum = dall[len(table_levels) * length:]
        qb = q.astype(BF16)
        kb = kk.astype(BF16)
        vb = v.astype(BF16)
        xs = []
        for lv in range(nlev):
            hs = length >> (lv + 1)
            if hs % SUBLANES == 0:
                qk, dl = [], []
                for i in range(length // hs):
                    part = slice(i * hs, (i + 1) * hs)
                    bnd = (i // 2) * 2 * hs + hs - 1
                    gb = gcum[bnd:bnd + 1]
                    qk.append((q if i % 2 else kk)[part])
                    dl.append(gcum[part] - gb if i % 2 else gb - gcum[part])
                x = jnp.exp(jnp.concatenate(dl, axis=0)) * jnp.concatenate(qk, axis=0)
            elif hs == 1:
                x = jnp.where(qm_ref[lv] != 0.0, q * f, kk)
            else:
                slab = table_levels.index(lv)
                x = jnp.exp(dall[slab * length:(slab + 1) * length]) * jnp.where(
                    qm_ref[lv] != 0.0, q, kk)
            xs.append(x.astype(BF16))
        qt = (q * jnp.exp(gcum)).astype(BF16)
        outs = []
        for pr in range(C_HEADS // 2):
            ps = slice(pr * pair_w, (pr + 1) * pair_w)
            amat = lax.dot_general(qb[:, ps], _pair_blocks(kb[:, ps]), _NT,
                                   preferred_element_type=F32) * pm_ref[nlev]
            for lv in range(nlev):
                xp = xs[lv][:, ps]
                amat = amat + lax.dot_general(xp, _pair_blocks(xp), _NT,
                                              preferred_element_type=F32) * pm_ref[lv]
            z128 = jnp.zeros((C_DIM, C_DIM), F32)
            st2 = jnp.concatenate(
                [jnp.concatenate([st_scr[2 * pr], z128], axis=1),
                 jnp.concatenate([z128, st_scr[2 * pr + 1]], axis=1)], axis=0).astype(BF16)
            outs.append(
                jnp.dot(amat.astype(BF16), _pair_blocks(vb[:, ps]), preferred_element_type=F32)
                + lax.dot_general(qt[:, ps], st2, _NT, preferred_element_type=F32))
        glast = gcum[length - 1:length, :]
        kdb = (kk * jnp.exp(glast - gcum)).astype(BF16)
        eg = jnp.exp(glast)
        for hd in range(C_HEADS):
            hsl = slice(hd * C_DIM, (hd + 1) * C_DIM)
            st_scr[hd] = st_scr[hd] * eg[:, hsl] + lax.dot_general(
                vb[:, hsl], kdb[:, hsl], _TN, preferred_element_type=F32)
        o = jnp.concatenate(outs, axis=1)
        inv = jnp.concatenate(
            [jnp.broadcast_to(
                lax.rsqrt(jnp.mean(jnp.square(o[:, hd * C_DIM:(hd + 1) * C_DIM]), axis=-1,
                                   keepdims=True) + EPS), (length, C_DIM))
             for hd in range(C_HEADS)], axis=1)
        gx = z_scr[rows, 3 * width:4 * width]
        og_ref[0, rows, :] = (o * inv * gn_ref[...] * (gx * jax.nn.sigmoid(gx))).astype(BF16)

    @pl.when(j == pl.num_programs(1) - 1)
    def _():
        for hd in range(C_HEADS):
            sout_ref[0, hd] = st_scr[hd].T


def _hgrn(x, g, w, lbp, s0, gn):
    b, t, d = x.shape
    width = w.shape[1] // 4
    assert width == C_HEADS * C_DIM
    tm = HGRN_ROWS if t % HGRN_ROWS == 0 else _pick_tm(t)
    length = min(CHUNK, t)
    nlev, table_levels, sums, qmask, pmask = _hgrn_tables(length, width)
    tile = lambda c: pl.BlockSpec((1, tm, c), lambda i, j: (i, j, 0))
    state = pl.BlockSpec((1, C_HEADS, C_DIM, C_DIM), lambda i, j: (i, 0, 0, 0))
    return pl.pallas_call(
        functools.partial(_hgrn_kernel, tm=tm, length=length, nlev=nlev,
                          table_levels=table_levels, width=width),
        grid=(b, t // tm),
        in_specs=[
            tile(d),
            _resident((1, d)),
            _resident(w.shape),
            _resident(lbp.shape),
            state,
            _resident((1, width)),
            _resident(sums.shape),
            _resident(qmask.shape),
            _resident(pmask.shape),
        ],
        out_specs=[tile(width), state],
        out_shape=[
            jax.ShapeDtypeStruct((b, t, width), BF16),
            jax.ShapeDtypeStruct((b, C_HEADS, C_DIM, C_DIM), F32),
        ],
        scratch_shapes=[
            pltpu.VMEM((tm, 4 * width), F32),
            pltpu.VMEM((C_HEADS, C_DIM, C_DIM), F32),
        ],
        compiler_params=_params(),
        name="inproj_hgrn",
    )(x, g, w, lbp, s0, jnp.tile(gn, (1, C_HEADS)), jnp.asarray(sums, BF16), jnp.asarray(qmask),
      jnp.asarray(pmask))


def _rel_bias_table(rel_bias):
    period = Q_ROWS + WIN
    offs = np.arange(period) - (Q_ROWS - 1)
    idx = np.clip(BAND_PAST - offs, -MAX_REL, MAX_REL) + MAX_REL
    u = rel_bias[:, idx].astype(F32)
    u = u - rel_bias[:, 2 * MAX_REL:].astype(F32)
    skew = jnp.tile(u, (1, Q_ROWS))[:, :Q_ROWS * (period - 1)].reshape(-1, Q_ROWS, period - 1)
    toep = skew[:, :, Q_ROWS - 1:Q_ROWS - 1 + WIN]
    first = (np.arange(Q_ROWS) // CHUNK * CHUNK)[:, None]
    col = np.arange(WIN)[None, :]
    return jnp.where((col >= first) & (col < first + BAND), toep, NEG)


def _stream(x, cache_k, cache_v, conv_hist, s0, p):
    b, t, d = x.shape
    bw = p["conv_w"].shape[-1]
    row = lambda a: a.reshape(1, -1)
    if cache_k is None:
        hist = jnp.zeros((b, HIST_PAD, bw), F32)
    else:
        hist = jnp.pad(conv_hist, ((0, 0), (HIST_OFF, 0), (0, 0)))
    q, k, v, kf, vf, co, ut = _inproj_ab(
        x, row(p["attn_norm"][0]), p["w_in_ab"], hist, p["conv_w"], row(p["conv_b"]),
        row(p["conv_ln_g"]), row(p["conv_ln_b"]))
    if cache_k is None:
        att = _attention(q, k, v, p["bias"], 0)
    else:
        w = cache_k.shape[1]
        assert w == BAND_PAST and t <= CHUNK
        pad = ((0, 0), (0, Q_ROWS - t), (0, 0))
        kp = jnp.concatenate([cache_k.reshape(b, w, A_WIDTH).astype(BF16), jnp.pad(k, pad)], axis=1)
        vp = jnp.concatenate([cache_v.reshape(b, w, A_WIDTH).astype(BF16), jnp.pad(v, pad)], axis=1)
        bias = jnp.where(np.arange(WIN)[None, None, :] < w + t, p["bias"], NEG)
        att = _attention(jnp.pad(q, pad), kp, vp, bias, w)[:, :t]
    y = _outproj_ffn(x, [att, co], p["w_out_ab"], row(p["ffn_norm"][0]), p["ffn_w_in"][0],
                     p["ffn_w_out"][0])
    keep = kf.shape[1]
    new_k = kf.reshape(1, b, keep, A_HEADS, A_HEAD_DIM)
    new_v = vf.reshape(1, b, keep, A_HEADS, A_HEAD_DIM)
    new_conv = ut[None, :, HIST_OFF:]
    if s0 is None:
        s0 = jnp.zeros((b, C_HEADS, C_DIM, C_DIM), F32)
    og, s_new = _hgrn(y, row(p["attn_norm"][1]), p["w_in_c"], p["lower_bounds"], s0,
                      row(p["c_norm_g"]))
    y = _outproj_ffn(y, [og], p["w_out_c"], row(p["ffn_norm"][1]), p["ffn_w_in"][1],
                     p["ffn_w_out"][1], fn=row(p["final_norm"]))
    return y, new_k, new_v, new_conv, s_new[None]


def kernel(x_prompt, x_sample, cache_a_k, cache_a_v, state_conv, state_hgrn, attn_norm, ffn_norm, final_norm, w_in_ab, rel_bias, conv_w, conv_b, conv_ln_g, conv_ln_b, w_out_ab, w_in_c, lower_bounds, c_norm_g, w_out_c, ffn_w_in, ffn_w_out):
    assert attn_norm.shape[0] == 2 and w_in_ab.shape[0] == 1 and w_in_c.shape[0] == 1
    p = dict(
        attn_norm=attn_norm, ffn_norm=ffn_norm, final_norm=final_norm,
        w_in_ab=w_in_ab[0].astype(BF16),
        bias=_rel_bias_table(rel_bias[0]),
        conv_w=jnp.broadcast_to(conv_w[0][:, None, :], (CONV_WIDTH, SUBLANES, conv_w.shape[-1])),
        conv_b=conv_b[0], conv_ln_g=conv_ln_g[0], conv_ln_b=conv_ln_b[0],
        w_out_ab=w_out_ab[0].astype(BF16),
        w_in_c=w_in_c[0].astype(BF16),
        lower_bounds=lower_bounds,
        c_norm_g=c_norm_g[0],
        w_out_c=w_out_c[0].astype(BF16),
        ffn_w_in=ffn_w_in.astype(BF16),
        ffn_w_out=ffn_w_out.astype(BF16),
    )
    yp, akp, avp, cvp, hgp = _stream(x_prompt, None, None, None, None, p)
    ys, aks, avs, cvs, hgs = _stream(x_sample, cache_a_k[0], cache_a_v[0], state_conv[0],
                                     state_hgrn[0], p)
    return (yp, ys, akp, avp, cvp, hgp, aks, avs, cvs, hgs)
```

```python
import functools

import numpy as np
import jax
import jax.numpy as jnp
from jax import lax
from jax.experimental import pallas as pl
from jax.experimental.pallas import tpu as pltpu

EPS = 1e-6
NEG = -1e30
CHUNK = 64
BAND_PAST = 512
BAND = BAND_PAST + CHUNK
Q_ROWS = 4 * CHUNK
WIN = BAND_PAST + Q_ROWS
A_HEADS = 8
A_HEAD_DIM = 64
A_WIDTH = A_HEADS * A_HEAD_DIM
MAX_REL = 128
MXU_COLS = 256
LANES = 128
SUBLANES = 8
BIAS_FREE = BAND_PAST - MAX_REL
CONV_WIDTH = 31
C_HEADS = 8
C_DIM = 128
HIST_ROWS = CONV_WIDTH - 1
HIST_PAD = 32
HIST_OFF = HIST_PAD - HIST_ROWS
CONV_ROWS = 32
FFN_ROWS = 512
HGRN_ROWS = 1024
INPROJ_ROWS = 1024
ATTN_GROUPS = 4
VMEM_LIMIT_BYTES = 56 * 1024 * 1024
BF16 = jnp.bfloat16
F32 = jnp.float32

_NT = (((1,), (1,)), ((), ()))
_TN = (((0,), (0,)), ((), ()))


def _rms(x, g):
    return x * lax.rsqrt(jnp.mean(x * x, axis=-1, keepdims=True) + EPS) * g


def _resident(shape):
    n = len(shape)
    return pl.BlockSpec(shape, lambda *_: (0,) * n, pipeline_mode=pl.Buffered(1))


def _params():
    return pltpu.CompilerParams(
        dimension_semantics=("arbitrary", "arbitrary"),
        vmem_limit_bytes=VMEM_LIMIT_BYTES,
    )


def _pick_tm(t):
    for tm in (256, 128, 64):
        if t % tm == 0:
            return tm
    return t


def _inproj_ab_kernel(x_ref, g_ref, w_ref, hist_ref, cw_ref, cb_ref, lng_ref, lnb_ref,
                      q_ref, k_ref, v_ref, kf_ref, vf_ref, co_ref, ut_ref, uext_ref,
                      cacc_ref, h_ref, *, tm, bw, rb):
    j = pl.program_id(1)
    h = _rms(x_ref[0], g_ref[...]).astype(BF16)
    zc = jnp.dot(h, w_ref[:, 3 * A_WIDTH:], preferred_element_type=F32)
    u = zc[:, :bw] * jax.nn.sigmoid(zc[:, bw:])

    @pl.when(j == 0)
    def _():
        uext_ref[0:HIST_PAD, :] = hist_ref[0]

    uext_ref[HIST_PAD:HIST_PAD + tm, :] = u
    h_ref[...] = h

    def project(c0, c1):
        z = jnp.dot(h_ref[...], w_ref[:, c0:c1], preferred_element_type=F32)
        if c0 < A_WIDTH:
            q_ref[0, :, c0:c1] = (z * (A_HEAD_DIM ** -0.5)).astype(BF16)
        elif c0 < 2 * A_WIDTH:
            k_ref[0, :, c0 - A_WIDTH:c1 - A_WIDTH] = z.astype(BF16)
            kf_ref[0, :, c0 - A_WIDTH:c1 - A_WIDTH] = z[tm - kf_ref.shape[1]:]
        else:
            v_ref[0, :, c0 - 2 * A_WIDTH:c1 - 2 * A_WIDTH] = z.astype(BF16)
            vf_ref[0, :, c0 - 2 * A_WIDTH:c1 - 2 * A_WIDTH] = z[tm - vf_ref.shape[1]:]

    slices = list(range(0, 3 * A_WIDTH, MXU_COLS))

    def conv_block(i):
        if i < len(slices):
            project(slices[i], slices[i] + MXU_COLS)
        r0 = i * rb
        for l0 in range(0, bw, LANES):
            lanes = slice(l0, l0 + LANES)
            x = uext_ref[r0:r0 + rb + HIST_PAD, lanes]
            acc = jnp.broadcast_to(cb_ref[:, lanes], (rb, LANES))
            for s in range(SUBLANES):
                xs = x if s == 0 else x[s:s + rb + HIST_PAD - SUBLANES]
                for a in range(HIST_PAD // SUBLANES + 1):
                    t = a * SUBLANES + s - HIST_OFF
                    if 0 <= t < CONV_WIDTH:
                        wt = jnp.concatenate([cw_ref[t, :, lanes]] * (rb // SUBLANES), axis=0)
                        acc = acc + wt * xs[a * SUBLANES:a * SUBLANES + rb]
            cacc_ref[r0:r0 + rb, lanes] = acc

    for i in range(tm // rb):
        pl.when(j >= 0)(functools.partial(conv_block, i))
    for c0 in slices[tm // rb:]:
        project(c0, c0 + MXU_COLS)
    y = cacc_ref[...]
    mu = jnp.mean(y, axis=-1, keepdims=True)
    d = y - mu
    var = jnp.mean(d * d, axis=-1, keepdims=True)
    y = d * lax.rsqrt(var + EPS) * lng_ref[...] + lnb_ref[...]
    co_ref[0] = (y * jax.nn.sigmoid(y)).astype(BF16)
    tail = uext_ref[tm:tm + HIST_PAD, :]
    ut_ref[0] = tail
    uext_ref[0:HIST_PAD, :] = tail


def _inproj_ab(x, g, w, hist, cw, cb, lng, lnb):
    b, t, d = x.shape
    n = w.shape[1]
    bw = (n - 3 * A_WIDTH) // 2
    tm = INPROJ_ROWS if t % INPROJ_ROWS == 0 else _pick_tm(t)
    nt = t // tm
    keep = min(BAND_PAST, t)
    keep_rows = min(keep, tm)
    first_keep = nt - keep // keep_rows
    rb = CONV_ROWS if tm % CONV_ROWS == 0 else tm
    tile = lambda c: pl.BlockSpec((1, tm, c), lambda i, j: (i, j, 0))
    keep_spec = pl.BlockSpec((1, keep_rows, A_WIDTH),
                             lambda i, j: (i, jnp.maximum(j - first_keep, 0), 0))
    return pl.pallas_call(
        functools.partial(_inproj_ab_kernel, tm=tm, bw=bw, rb=rb),
        grid=(b, nt),
        in_specs=[
            tile(d),
            _resident((1, d)),
            _resident((d, n)),
            pl.BlockSpec((1, HIST_PAD, bw), lambda i, j: (i, 0, 0)),
            _resident((CONV_WIDTH, SUBLANES, bw)),
            _resident((1, bw)),
            _resident((1, bw)),
            _resident((1, bw)),
        ],
        out_specs=[
            tile(A_WIDTH), tile(A_WIDTH), tile(A_WIDTH),
            keep_spec, keep_spec,
            tile(bw),
            pl.BlockSpec((1, HIST_PAD, bw), lambda i, j: (i, 0, 0)),
        ],
        out_shape=[
            jax.ShapeDtypeStruct((b, t, A_WIDTH), BF16),
            jax.ShapeDtypeStruct((b, t, A_WIDTH), BF16),
            jax.ShapeDtypeStruct((b, t, A_WIDTH), BF16),
            jax.ShapeDtypeStruct((b, keep, A_WIDTH), F32),
            jax.ShapeDtypeStruct((b, keep, A_WIDTH), F32),
            jax.ShapeDtypeStruct((b, t, bw), BF16),
            jax.ShapeDtypeStruct((b, HIST_PAD, bw), F32),
        ],
        scratch_shapes=[
            pltpu.VMEM((tm + HIST_PAD, bw), F32),
            pltpu.VMEM((tm, bw), F32),
            pltpu.VMEM((tm, d), BF16),
        ],
        compiler_params=_params(),
        name="inproj_ab",
    )(x, g, w, hist, cw, cb, lng, lnb)


def _attn_group(q_ref, k_ref, v_ref, b_ref, o_ref, kstart, wlen, r0):
    q = q_ref[0, r0:r0 + Q_ROWS, :]
    kw = k_ref[0, pl.ds(kstart, wlen), :]
    vw = v_ref[0, pl.ds(kstart, wlen), :]
    off = WIN - wlen
    low = lax.broadcasted_iota(jnp.int32, (1, LANES), 1) < A_HEAD_DIM
    head_lanes = (low.astype(BF16), (~low).astype(BF16))
    ones = jnp.ones((wlen, LANES), BF16)
    outs = []
    for h in range(A_HEADS):
        sl = slice(h // 2 * LANES, (h // 2 + 1) * LANES)
        s = lax.dot_general(q[:, sl] * head_lanes[h % 2], kw[:, sl], _NT,
                            preferred_element_type=F32)
        probs = []
        for g in range(Q_ROWS // CHUNK):
            rows = slice(g * CHUNK, (g + 1) * CHUNK)
            lo = g * CHUNK
            tiles = []
            for c0 in range(off, WIN, LANES):
                c1 = c0 + LANES
                if c1 <= lo or c0 >= lo + BAND:
                    tiles.append(None)
                    continue
                st = s[rows, c0 - off:c1 - off]
                if c0 < lo or c1 > lo + BIAS_FREE:
                    st = st + b_ref[h, rows, c0:c1]
                tiles.append(st)
            live = [t for t in tiles if t is not None]
            m = jnp.max(functools.reduce(jnp.maximum, live), axis=-1, keepdims=True)
            probs.append(jnp.concatenate(
                [jnp.zeros((CHUNK, LANES), BF16) if t is None else jnp.exp(t - m).astype(BF16)
                 for t in tiles], axis=1))
        p = jnp.concatenate(probs, axis=0)
        pv = jnp.dot(p, jnp.concatenate([vw[:, sl], ones], axis=1), preferred_element_type=F32)
        o = pv[:, :LANES] / pv[:, LANES:]
        if h % 2:
            outs.append(jnp.where(low, outs.pop(), o))
        else:
            outs.append(o)
    o_ref[0, r0:r0 + Q_ROWS, :] = jnp.concatenate(outs, axis=-1).astype(BF16)


def _attn_kernel(q_ref, k_ref, v_ref, b_ref, o_ref, *, lead, gps):
    step = pl.program_id(1)
    n_short = (BAND_PAST - lead) // Q_ROWS
    n_special = -(-n_short // gps)

    def group(gi, sub):
        if isinstance(gi, int) and gi < n_short:
            kstart, wlen = 0, lead + (gi + 1) * Q_ROWS
        else:
            kstart, wlen = (gi - n_short) * Q_ROWS, WIN
            if not isinstance(gi, int):
                kstart = pl.multiple_of(kstart, Q_ROWS)
        _attn_group(q_ref, k_ref, v_ref, b_ref, o_ref, kstart, wlen, sub * Q_ROWS)

    for s0 in range(n_special):
        @pl.when(step == s0)
        def _(s0=s0):
            for sub in range(gps):
                group(s0 * gps + sub, sub)

    @pl.when(step >= n_special)
    def _():
        for sub in range(gps):
            group(step * gps + sub, sub)


def _attention(q, k, v, bias, lead):
    b, tq, _ = q.shape
    tk = k.shape[1]
    assert tq % Q_ROWS == 0 and tk == lead + tq and (BAND_PAST - lead) % Q_ROWS == 0
    gps = ATTN_GROUPS if (tq // Q_ROWS) % ATTN_GROUPS == 0 else 1
    return pl.pallas_call(
        functools.partial(_attn_kernel, lead=lead, gps=gps),
        grid=(b, tq // (gps * Q_ROWS)),
        in_specs=[
            pl.BlockSpec((1, gps * Q_ROWS, A_WIDTH), lambda i, j: (i, j, 0)),
            pl.BlockSpec((1, tk, A_WIDTH), lambda i, j: (i, 0, 0)),
            pl.BlockSpec((1, tk, A_WIDTH), lambda i, j: (i, 0, 0)),
            _resident((A_HEADS, Q_ROWS, WIN)),
        ],
        out_specs=pl.BlockSpec((1, gps * Q_ROWS, A_WIDTH), lambda i, j: (i, j, 0)),
        out_shape=jax.ShapeDtypeStruct((b, tq, A_WIDTH), BF16),
        compiler_params=_params(),
        name="band_attention",
    )(q, k, v, bias)


def _outproj_ffn_kernel(*refs, n_in, final):
    y_ref = refs[0]
    ins = refs[1:1 + n_in]
    wo_ref, fg_ref, w1_ref, w2_ref = refs[1 + n_in:5 + n_in]
    fn_ref = refs[5 + n_in] if final else None
    o_ref = refs[-1]
    y = y_ref[0]
    r0 = 0
    for in_ref in ins:
        r1 = r0 + in_ref.shape[-1]
        y = y + jnp.dot(in_ref[0], wo_ref[r0:r1, :], preferred_element_type=F32)
        r0 = r1
    hn = _rms(y, fg_ref[...]).astype(BF16)
    hh = jnp.dot(hn, w1_ref[...], preferred_element_type=F32)
    dff = w2_ref.shape[0]
    a = hh[:, :dff]
    act = (a * jax.nn.sigmoid(a) * hh[:, dff:]).astype(BF16)
    y = y + jnp.dot(act, w2_ref[...], preferred_element_type=F32)
    if final:
        y = _rms(y, fn_ref[...])
    o_ref[0] = y


def _outproj_ffn(y, ins, wo, fg, w1, w2, fn=None):
    b, t, d = y.shape
    assert sum(a.shape[-1] for a in ins) == wo.shape[0]
    tm = FFN_ROWS if t % FFN_ROWS == 0 else _pick_tm(t)
    tile = lambda c: pl.BlockSpec((1, tm, c), lambda i, j: (i, j, 0))
    in_specs = [tile(d)] + [tile(a.shape[-1]) for a in ins]
    args = [y] + list(ins)
    in_specs += [_resident(wo.shape), _resident((1, d)), _resident(w1.shape), _resident(w2.shape)]
    args += [wo, fg, w1, w2]
    if fn is not None:
        in_specs.append(_resident((1, d)))
        args.append(fn)
    return pl.pallas_call(
        functools.partial(_outproj_ffn_kernel, n_in=len(ins), final=fn is not None),
        grid=(b, t // tm),
        in_specs=in_specs,
        out_specs=tile(d),
        out_shape=jax.ShapeDtypeStruct((b, t, d), F32),
        compiler_params=_params(),
        name="outproj_ffn",
    )(*args)


def _hgrn_tables(length, width):
    nlev = int(np.log2(length))
    assert 2 ** nlev == length
    r = np.arange(length)
    sums = np.zeros((nlev + 1, length, length), np.float32)
    qmask = np.zeros((nlev, length, width), np.float32)
    pmask = np.zeros((nlev + 1, length, length), np.float32)
    for lv in range(nlev):
        hs = length >> (lv + 1)
        blk = r // (2 * hs)
        bnd = blk * 2 * hs + hs - 1
        is_q = (r % (2 * hs)) >= hs
        c = r[None, :]
        q_rows = (c > bnd[:, None]) & (c <= r[:, None])
        k_rows = (c > r[:, None]) & (c <= bnd[:, None])
        sums[lv] = np.where(is_q[:, None], q_rows, k_rows)
        qmask[lv] = is_q[:, None]
        pmask[lv] = (blk[:, None] == blk[None, :]) & is_q[:, None] & ~is_q[None, :]
    sums[nlev] = r[None, :] <= r[:, None]
    pmask[nlev] = np.eye(length)
    table_levels = [lv for lv in range(nlev) if 1 < (length >> (lv + 1)) < SUBLANES]
    sums = sums[table_levels + [nlev]].reshape(-1, length)
    return (nlev, table_levels, np.concatenate([sums, sums], axis=1), qmask,
            np.concatenate([pmask, pmask], axis=2))


def _pair_blocks(a):
    z = jnp.zeros((a.shape[0], C_DIM), a.dtype)
    return jnp.concatenate([jnp.concatenate([a[:, :C_DIM], z], axis=1),
                            jnp.concatenate([z, a[:, C_DIM:]], axis=1)], axis=0)


def _hgrn_kernel(x_ref, g_ref, w_ref, lbp_ref, s0_ref, gn_ref, sums_ref, qm_ref, pm_ref,
                 og_ref, sout_ref, z_scr, st_scr, *, tm, length, nlev, table_levels, width):
    j = pl.program_id(1)
    h = _rms(x_ref[0], g_ref[...]).astype(BF16)
    z_scr[...] = jnp.dot(h, w_ref[...], preferred_element_type=F32)

    @pl.when(j == 0)
    def _():
        for hd in range(C_HEADS):
            st_scr[hd] = s0_ref[0, hd].T

    lbp = lbp_ref[...]
    mx = jnp.max(lbp, axis=0, keepdims=True)
    e = jnp.exp(lbp - mx)
    den = e[0:1] + e[1:2]
    s_first = e[0:1] / den
    lb = (s_first + e[1:2] / den) - s_first
    pair_w = 2 * C_DIM

    for r0 in range(0, tm, length):
        rows = slice(r0, r0 + length)
        q = z_scr[rows, 0:width]
        f = lb + (1.0 - lb) * jax.nn.sigmoid(z_scr[rows, width:2 * width])
        v = z_scr[rows, 2 * width:3 * width]
        logf = jnp.log(f)
        kk = 1.0 - f
        hi = logf.astype(BF16)
        mid = (logf - hi.astype(F32)).astype(BF16)
        dall = jnp.dot(sums_ref[...], jnp.concatenate([hi, mid], axis=0),
                       preferred_element_type=F32)
        gcum = dall[len(table_levels) * length:]
        qb = q.astype(BF16)
        kb = kk.astype(BF16)
        vb = v.astype(BF16)
        xs = []
        for lv in range(nlev):
            hs = length >> (lv + 1)
            if hs % SUBLANES == 0:
                qk, dl = [], []
                for i in range(length // hs):
                    part = slice(i * hs, (i + 1) * hs)
                    bnd = (i // 2) * 2 * hs + hs - 1
                    gb = gcum[bnd:bnd + 1]
                    qk.append((q if i % 2 else kk)[part])
                    dl.append(gcum[part] - gb if i % 2 else gb - gcum[part])
                x = jnp.exp(jnp.concatenate(dl, axis=0)) * jnp.concatenate(qk, axis=0)
            elif hs == 1:
                x = jnp.where(qm_ref[lv] != 0.0, q * f, kk)
            else:
                slab = table_levels.index(lv)
                x = jnp.exp(dall[slab * length:(slab + 1) * length]) * jnp.where(
                    qm_ref[lv] != 0.0, q, kk)
            xs.append(x.astype(BF16))
        qt = (q * jnp.exp(gcum)).astype(BF16)
        outs = []
        for pr in range(C_HEADS // 2):
            ps = slice(pr * pair_w, (pr + 1) * pair_w)
            amat = lax.dot_general(qb[:, ps], _pair_blocks(kb[:, ps]), _NT,
                                   preferred_element_type=F32) * pm_ref[nlev]
            for lv in range(nlev):
                xp = xs[lv][:, ps]
                amat = amat + lax.dot_general(xp, _pair_blocks(xp), _NT,
                                              preferred_element_type=F32) * pm_ref[lv]
            z128 = jnp.zeros((C_DIM, C_DIM), F32)
            st2 = jnp.concatenate(
                [jnp.concatenate([st_scr[2 * pr], z128], axis=1),
                 jnp.concatenate([z128, st_scr[2 * pr + 1]], axis=1)], axis=0).astype(BF16)
            outs.append(
                jnp.dot(amat.astype(BF16), _pair_blocks(vb[:, ps]), preferred_element_type=F32)
                + lax.dot_general(qt[:, ps], st2, _NT, preferred_element_type=F32))
        glast = gcum[length - 1:length, :]
        kdb = (kk * jnp.exp(glast - gcum)).astype(BF16)
        eg = jnp.exp(glast)
        for hd in range(C_HEADS):
            hsl = slice(hd * C_DIM, (hd + 1) * C_DIM)
            st_scr[hd] = st_scr[hd] * eg[:, hsl] + lax.dot_general(
                vb[:, hsl], kdb[:, hsl], _TN, preferred_element_type=F32)
        o = jnp.concatenate(outs, axis=1)
        inv = jnp.concatenate(
            [jnp.broadcast_to(
                lax.rsqrt(jnp.mean(jnp.square(o[:, hd * C_DIM:(hd + 1) * C_DIM]), axis=-1,
                                   keepdims=True) + EPS), (length, C_DIM))
             for hd in range(C_HEADS)], axis=1)
        gx = z_scr[rows, 3 * width:4 * width]
        og_ref[0, rows, :] = (o * inv * gn_ref[...] * (gx * jax.nn.sigmoid(gx))).astype(BF16)

    @pl.when(j == pl.num_programs(1) - 1)
    def _():
        for hd in range(C_HEADS):
            sout_ref[0, hd] = st_scr[hd].T


def _hgrn(x, g, w, lbp, s0, gn):
    b, t, d = x.shape
    width = w.shape[1] // 4
    assert width == C_HEADS * C_DIM
    tm = HGRN_ROWS if t % HGRN_ROWS == 0 else _pick_tm(t)
    length = min(CHUNK, t)
    nlev, table_levels, sums, qmask, pmask = _hgrn_tables(length, width)
    tile = lambda c: pl.BlockSpec((1, tm, c), lambda i, j: (i, j, 0))
    state = pl.BlockSpec((1, C_HEADS, C_DIM, C_DIM), lambda i, j: (i, 0, 0, 0))
    return pl.pallas_call(
        functools.partial(_hgrn_kernel, tm=tm, length=length, nlev=nlev,
                          table_levels=table_levels, width=width),
        grid=(b, t // tm),
        in_specs=[
            tile(d),
            _resident((1, d)),
            _resident(w.shape),
            _resident(lbp.shape),
            state,
            _resident((1, width)),
            _resident(sums.shape),
            _resident(qmask.shape),
            _resident(pmask.shape),
        ],
        out_specs=[tile(width), state],
        out_shape=[
            jax.ShapeDtypeStruct((b, t, width), BF16),
            jax.ShapeDtypeStruct((b, C_HEADS, C_DIM, C_DIM), F32),
        ],
        scratch_shapes=[
            pltpu.VMEM((tm, 4 * width), F32),
            pltpu.VMEM((C_HEADS, C_DIM, C_DIM), F32),
        ],
        compiler_params=_params(),
        name="inproj_hgrn",
    )(x, g, w, lbp, s0, jnp.tile(gn, (1, C_HEADS)), jnp.asarray(sums, BF16), jnp.asarray(qmask),
      jnp.asarray(pmask))


def _rel_bias_table(rel_bias):
    period = Q_ROWS + WIN
    offs = np.arange(period) - (Q_ROWS - 1)
    idx = np.clip(BAND_PAST - offs, -MAX_REL, MAX_REL) + MAX_REL
    u = rel_bias[:, idx].astype(F32)
    u = u - rel_bias[:, 2 * MAX_REL:].astype(F32)
    skew = jnp.tile(u, (1, Q_ROWS))[:, :Q_ROWS * (period - 1)].reshape(-1, Q_ROWS, period - 1)
    toep = skew[:, :, Q_ROWS - 1:Q_ROWS - 1 + WIN]
    first = (np.arange(Q_ROWS) // CHUNK * CHUNK)[:, None]
    col = np.arange(WIN)[None, :]
    return jnp.where((col >= first) & (col < first + BAND), toep, NEG)


def _stream(x, cache_k, cache_v, conv_hist, s0, p):
    b, t, d = x.shape
    bw = p["conv_w"].shape[-1]
    row = lambda a: a.reshape(1, -1)
    if cache_k is None:
        hist = jnp.zeros((b, HIST_PAD, bw), F32)
    else:
        hist = jnp.pad(conv_hist, ((0, 0), (HIST_OFF, 0), (0, 0)))
    q, k, v, kf, vf, co, ut = _inproj_ab(
        x, row(p["attn_norm"][0]), p["w_in_ab"], hist, p["conv_w"], row(p["conv_b"]),
        row(p["conv_ln_g"]), row(p["conv_ln_b"]))
    if cache_k is None:
        att = _attention(q, k, v, p["bias"], 0)
    else:
        w = cache_k.shape[1]
        assert w == BAND_PAST and t <= CHUNK
        pad = ((0, 0), (0, Q_ROWS - t), (0, 0))
        kp = jnp.concatenate([cache_k.reshape(b, w, A_WIDTH).astype(BF16), jnp.pad(k, pad)], axis=1)
        vp = jnp.concatenate([cache_v.reshape(b, w, A_WIDTH).astype(BF16), jnp.pad(v, pad)], axis=1)
        bias = jnp.where(np.arange(WIN)[None, None, :] < w + t, p["bias"], NEG)
        att = _attention(jnp.pad(q, pad), kp, vp, bias, w)[:, :t]
    y = _outproj_ffn(x, [att, co], p["w_out_ab"], row(p["ffn_norm"][0]), p["ffn_w_in"][0],
                     p["ffn_w_out"][0])
    keep = kf.shape[1]
    new_k = kf.reshape(1, b, keep, A_HEADS, A_HEAD_DIM)
    new_v = vf.reshape(1, b, keep, A_HEADS, A_HEAD_DIM)
    new_conv = ut[None, :, HIST_OFF:]
    if s0 is None:
        s0 = jnp.zeros((b, C_HEADS, C_DIM, C_DIM), F32)
    og, s_new = _hgrn(y, row(p["attn_norm"][1]), p["w_in_c"], p["lower_bounds"], s0,
                      row(p["c_norm_g"]))
    y = _outproj_ffn(y, [og], p["w_out_c"], row(p["ffn_norm"][1]), p["ffn_w_in"][1],
                     p["ffn_w_out"][1], fn=row(p["final_norm"]))
    return y, new_k, new_v, new_conv, s_new[None]


def kernel(x_prompt, x_sample, cache_a_k, cache_a_v, state_conv, state_hgrn, attn_norm, ffn_norm, final_norm, w_in_ab, rel_bias, conv_w, conv_b, conv_ln_g, conv_ln_b, w_out_ab, w_in_c, lower_bounds, c_norm_g, w_out_c, ffn_w_in, ffn_w_out):
    assert attn_norm.shape[0] == 2 and w_in_ab.shape[0] == 1 and w_in_c.shape[0] == 1
    p = dict(
        attn_norm=attn_norm, ffn_norm=ffn_norm, final_norm=final_norm,
        w_in_ab=w_in_ab[0].astype(BF16),
        bias=_rel_bias_table(rel_bias[0]),
        conv_w=jnp.broadcast_to(conv_w[0][:, None, :], (CONV_WIDTH, SUBLANES, conv_w.shape[-1])),
        conv_b=conv_b[0], conv_ln_g=conv_ln_g[0], conv_ln_b=conv_ln_b[0],
        w_out_ab=w_out_ab[0].astype(BF16),
        w_in_c=w_in_c[0].astype(BF16),
        lower_bounds=lower_bounds,
        c_norm_g=c_norm_g[0],
        w_out_c=w_out_c[0].astype(BF16),
        ffn_w_in=ffn_w_in.astype(BF16),
        ffn_w_out=ffn_w_out.astype(BF16),
    )
    yp, akp, avp, cvp, hgp = _stream(x_prompt, None, None, None, None, p)
    ys, aks, avs, cvs, hgs = _stream(x_sample, cache_a_k[0], cache_a_v[0], state_conv[0],
                                     state_hgrn[0], p)
    return (yp, ys, akp, avp, cvp, hgp, aks, avs, cvs, hgs)
```
